```python
import math
import jax, jax.numpy as jnp
from jax import lax
import numpy as np

D_MODEL = 1024
BATCH = 16
SEQ = 2048
DEPTH = 4

GRID_W = 64
CTX_LEN = 256
A_WIDTH = D_MODEL // 4
A_GROUPS = 4
A_GDIM = A_WIDTH // A_GROUPS
CHUNK = 128
B_HEAD_DIM = 64
B_HEADS = (D_MODEL // 2) // B_HEAD_DIM
B_KV_HEADS = B_HEADS // 4
B_WIDTH = B_HEADS * B_HEAD_DIM
WINDOW = 128
BLOCK = 128
C_V_DIM = 64
C_QK_DIM = C_V_DIM // 2
C_HEADS = (D_MODEL // 4) // C_V_DIM
C_WIDTH = C_HEADS * C_V_DIM
Q_BLOCK = 128
MIX_WIDTH = A_WIDTH + B_WIDTH + C_WIDTH
PROJ_SIZES = (2 * A_WIDTH, B_WIDTH, B_KV_HEADS * B_HEAD_DIM, B_KV_HEADS * B_HEAD_DIM,
              C_HEADS * 2 * C_QK_DIM, C_HEADS * 2 * C_QK_DIM, C_WIDTH)
PROJ_WIDTH = sum(PROJ_SIZES)
D_FF = 256 * ((8 * D_MODEL // 3 + 255) // 256)
CONV_W = 3
ROPE_BASE = 10000.0
EPS = 1e-6

kernel_name = 'hybrid_parallel_heads_diffusion_trunk'


def rms_norm(x, gain=None):
    xf = x.astype(jnp.float32)
    y = xf * lax.rsqrt(jnp.mean(xf * xf, axis=-1, keepdims=True) + EPS)
    if gain is not None:
        y = y * gain.astype(jnp.float32)
    return y.astype(x.dtype)


def modulation(cond, w_ada, b_ada):
    m = jax.nn.silu(cond) @ w_ada + b_ada
    return [t[..., None, :] for t in jnp.split(m, 6, axis=-1)]


def modulate(h, shift, scale):
    return h * (1 + scale) + shift


def split_proj(z):
    idx = np.cumsum(np.array(PROJ_SIZES))[:-1].tolist()
    return jnp.split(z, idx, axis=-1)


def axial_rope(n_tokens, head_dim):
    rows = n_tokens // GRID_W
    row = jnp.repeat(jnp.arange(rows, dtype=jnp.float32), GRID_W)
    col = jnp.tile(jnp.arange(GRID_W, dtype=jnp.float32), rows)
    n_freq = head_dim // 4
    inv_freq = ROPE_BASE ** (-jnp.arange(n_freq, dtype=jnp.float32) / n_freq)
    ang = jnp.stack([row[:, None] * inv_freq, col[:, None] * inv_freq], axis=1)
    return jnp.cos(ang), jnp.sin(ang)


def apply_rope(x, cos, sin):
    b, l, h, d = x.shape
    xr = x.reshape(b, l, h, 2, 2, d // 4).astype(jnp.float32)
    x1, x2 = xr[..., 0, :], xr[..., 1, :]
    c = cos[None, :, None]
    s = sin[None, :, None]
    out = jnp.stack([x1 * c - x2 * s, x2 * c + x1 * s], axis=-2)
    return out.reshape(b, l, h, d).astype(x.dtype)


def rope_maps(t, cos, sin):
    b, l, h, m, d = t.shape
    return apply_rope(t.reshape(b, l, h * m, d), cos, sin).reshape(b, l, h, m, d)


def chunk_gmlp(z, w_s, b_s):
    b, l, _ = z.shape
    z = jax.nn.gelu(z)
    u, v = jnp.split(z, 2, axis=-1)
    v = rms_norm(v.reshape(b, l // CHUNK, CHUNK, A_GROUPS, A_GDIM))
    mixed = jnp.einsum('gpq,bnqgc->bnpgc', w_s, v) + b_s.T[None, None, :, :, None]
    return u * mixed.reshape(b, l, A_WIDTH)


def window_attn_latent(q, k, v, k_ctx, v_ctx, sink):
    b, l, hq, d = q.shape
    hkv = k.shape[2]
    g = hq // hkv
    nb = l // BLOCK
    lc = k_ctx.shape[1]
    scale = d ** -0.5
    qb = q.reshape(b, nb, BLOCK, hkv, g, d)

    def band(t):
        tb = jnp.pad(t.reshape(b, nb, BLOCK, hkv, d), ((0, 0), (1, 1), (0, 0), (0, 0), (0, 0)))
        return jnp.concatenate([tb[:, :-2], tb[:, 1:-1], tb[:, 2:]], axis=2)

    kb, vb = band(k), band(v)
    s_win = jnp.einsum('bnqhgd,bnkhd->bnhgqk', qb, kb).astype(jnp.float32) * scale
    qpos = jnp.arange(nb)[:, None, None] * BLOCK + jnp.arange(BLOCK)[None, :, None]
    kpos = (jnp.arange(nb)[:, None, None] - 1) * BLOCK + jnp.arange(3 * BLOCK)[None, None, :]
    valid = (kpos >= 0) & (kpos < l) & (jnp.abs(qpos - kpos) <= WINDOW)
    s_win = jnp.where(valid[None, :, None, None], s_win, -jnp.inf)
    s_ctx = jnp.einsum('bnqhgd,bkhd->bnhgqk', qb, k_ctx).astype(jnp.float32) * scale
    s_sink = jnp.broadcast_to(sink.astype(jnp.float32).reshape(hkv, g)[None, None, :, :, None, None],
                              s_win.shape[:-1] + (1,))
    p = jax.nn.softmax(jnp.concatenate([s_win, s_ctx, s_sink], axis=-1), axis=-1)
    kw = 3 * BLOCK
    p_win = p[..., :kw].astype(v.dtype)
    p_ctx = p[..., kw:kw + lc].astype(v.dtype)
    o = (jnp.einsum('bnhgqk,bnkhd->bnqhgd', p_win, vb)
         + jnp.einsum('bnhgqk,bkhd->bnqhgd', p_ctx, v_ctx))
    return o.reshape(b, l, hq * d)


def sink_attn_ctx(q, k, v, sink):
    b, lc, hq, d = q.shape
    hkv = k.shape[2]
    g = hq // hkv
    qg = q.reshape(b, lc, hkv, g, d)
    s = jnp.einsum('bqhgd,bkhd->bhgqk', qg, k).astype(jnp.float32) * d ** -0.5
    s_sink = jnp.broadcast_to(sink.astype(jnp.float32).reshape(1, hkv, g, 1, 1), (b, hkv, g, lc, 1))
    p = jax.nn.softmax(jnp.concatenate([s, s_sink], axis=-1), axis=-1)[..., :lc]
    o = jnp.einsum('bhgqk,bkhd->bqhgd', p.astype(v.dtype), v)
    return o.reshape(b, lc, hq * d)


def diff_lambda(lam_params, lam_init):
    lp = lam_params.astype(jnp.float32)
    return jnp.exp(jnp.sum(lp[0] * lp[1])) - jnp.exp(jnp.sum(lp[2] * lp[3])) + lam_init


def diff_attn_block(q, k, v, lam):
    s = jnp.einsum('bqhmd,bkhmd->bhmqk', q, k).astype(jnp.float32) * q.shape[-1] ** -0.5
    p = jax.nn.softmax(s, axis=-1)
    pd = p[:, :, 0] - lam * p[:, :, 1]
    return jnp.einsum('bhqk,bkhd->bqhd', pd.astype(v.dtype), v)


def diff_attn_latent(q, k, v, k_ctx, v_ctx, lam):
    b, l, h, _, dq = q.shape
    k_all = jnp.concatenate([k, k_ctx], axis=1)
    v_all = jnp.concatenate([v, v_ctx], axis=1)
    qb = jnp.moveaxis(q.reshape(b, l // Q_BLOCK, Q_BLOCK, h, 2, dq), 1, 0)
    o = lax.map(lambda qblk: diff_attn_block(qblk, k_all, v_all, lam), qb)
    return jnp.moveaxis(o, 0, 1).reshape(b, l, h, -1)


def conv_ffn(h, w_up, conv_w, conv_b, w_down):
    a = h @ w_up
    a = lax.conv_general_dilated(a, conv_w, window_strides=(1,), padding=((CONV_W // 2, CONV_W // 2),),
                                 dimension_numbers=('NWC', 'WIO', 'NWC'),
                                 feature_group_count=a.shape[-1]) + conv_b
    gate, val = jnp.split(a, 2, axis=-1)
    return (jax.nn.silu(gate) * val) @ w_down


def setup_inputs(seed: int = 0) -> dict:
    key = jax.random.key(seed)
    ks = jax.random.split(key, 24)
    nrm = jax.random.normal
    f32 = jnp.float32
    L = DEPTH
    D = D_MODEL
    return {
        'x': nrm(ks[0], (BATCH, SEQ, D), f32),
        'c': nrm(ks[1], (BATCH, D), f32),
        'ctx': nrm(ks[2], (BATCH, CTX_LEN, D), f32),
        'c_ctx': nrm(ks[3], (D,), f32),
        'w_ada': nrm(ks[4], (L, D, 6 * D), f32) * (0.25 * D ** -0.5),
        'b_ada': nrm(ks[5], (L, 6 * D), f32) * 0.01,
        'norm1_g': 1.0 + 0.02 * nrm(ks[6], (L, D), f32),
        'norm2_g': 1.0 + 0.02 * nrm(ks[7], (L, D), f32),
        'w_in': nrm(ks[8], (L, D, PROJ_WIDTH), f32) * D ** -0.5,
        'a_ws': nrm(ks[9], (L, A_GROUPS, CHUNK, CHUNK), f32) * CHUNK ** -0.5,
        'a_bs': 1.0 + 0.02 * nrm(ks[10], (L, A_GROUPS, CHUNK), f32),
        'b_qnorm': 1.0 + 0.02 * nrm(ks[11], (L, B_HEAD_DIM), f32),
        'b_knorm': 1.0 + 0.02 * nrm(ks[12], (L, B_HEAD_DIM), f32),
        'b_sink': 0.5 * nrm(ks[13], (L, B_HEADS), f32),
        'c_qnorm': 1.0 + 0.02 * nrm(ks[14], (L, C_QK_DIM), f32),
        'c_knorm': 1.0 + 0.02 * nrm(ks[15], (L, C_QK_DIM), f32),
        'c_lam': 0.1 * nrm(ks[16], (L, 4, C_QK_DIM), f32),
        'c_subln': 1.0 + 0.02 * nrm(ks[17], (L, C_V_DIM), f32),
        'w_out': nrm(ks[18], (L, MIX_WIDTH, D), f32) * MIX_WIDTH ** -0.5,
        'w_up': nrm(ks[19], (L, D, 2 * D_FF), f32) * D ** -0.5,
        'conv_w': nrm(ks[20], (L, CONV_W, 1, 2 * D_FF), f32) * CONV_W ** -0.5,
        'conv_b': 0.01 * nrm(ks[21], (L, 2 * D_FF), f32),
        'w_down': nrm(ks[22], (L, D_FF, D), f32) * D_FF ** -0.5,
    }


def reference(x, c, ctx, c_ctx, w_ada, b_ada, norm1_g, norm2_g, w_in, a_ws, a_bs,
              b_qnorm, b_knorm, b_sink, c_qnorm, c_knorm, c_lam, c_subln,
              w_out, w_up, conv_w, conv_b, w_down):
    b, l, _ = x.shape
    lc = ctx.shape[1]
    cos_b, sin_b = axial_rope(l, B_HEAD_DIM)
    cos_c, sin_c = axial_rope(l, C_QK_DIM)
    for i in range(DEPTH):
        last = i == DEPTH - 1
        lam_init = 0.8 - 0.6 * math.exp(-0.3 * i)
        sh1, sc1, g1, sh2, sc2, g2 = modulation(c, w_ada[i], b_ada[i])
        csh1, csc1, cg1, csh2, csc2, cg2 = modulation(c_ctx, w_ada[i], b_ada[i])

        h = modulate(rms_norm(x, norm1_g[i]), sh1, sc1)
        hc = modulate(rms_norm(ctx, norm1_g[i]), csh1, csc1)
        a_z, bq, bk, bv, cq, ck, cv = split_proj(h @ w_in[i])
        a_zc, bqc, bkc, bvc, cqc, ckc, cvc = split_proj(hc @ w_in[i])

        k_bc = rms_norm(bkc.reshape(b, lc, B_KV_HEADS, B_HEAD_DIM), b_knorm[i])
        v_bc = bvc.reshape(b, lc, B_KV_HEADS, B_HEAD_DIM)
        k_cc = rms_norm(ckc.reshape(b, lc, C_HEADS, 2, C_QK_DIM), c_knorm[i])
        v_cc = cvc.reshape(b, lc, C_HEADS, C_V_DIM)
        lam = diff_lambda(c_lam[i], lam_init)

        out_a = chunk_gmlp(a_z, a_ws[i], a_bs[i])
        q_b = apply_rope(rms_norm(bq.reshape(b, l, B_HEADS, B_HEAD_DIM), b_qnorm[i]), cos_b, sin_b)
        k_b = apply_rope(rms_norm(bk.reshape(b, l, B_KV_HEADS, B_HEAD_DIM), b_knorm[i]), cos_b, sin_b)
        v_b = bv.reshape(b, l, B_KV_HEADS, B_HEAD_DIM)
        out_b = window_attn_latent(q_b, k_b, v_b, k_bc, v_bc, b_sink[i])
        q_c = rope_maps(rms_norm(cq.reshape(b, l, C_HEADS, 2, C_QK_DIM), c_qnorm[i]), cos_c, sin_c)
        k_c = rope_maps(rms_norm(ck.reshape(b, l, C_HEADS, 2, C_QK_DIM), c_knorm[i]), cos_c, sin_c)
        v_c = cv.reshape(b, l, C_HEADS, C_V_DIM)
        o_c = diff_attn_latent(q_c, k_c, v_c, k_cc, v_cc, lam)
        out_c = (rms_norm(o_c, c_subln[i]) * (1 - lam_init)).reshape(b, l, C_WIDTH)

        x = x + g1 * (jnp.concatenate([out_a, out_b, out_c], axis=-1) @ w_out[i])
        x = x + g2 * conv_ffn(modulate(rms_norm(x, norm2_g[i]), sh2, sc2),
                              w_up[i], conv_w[i], conv_b[i], w_down[i])

        if not last:
            out_ac = chunk_gmlp(a_zc, a_ws[i], a_bs[i])
            q_bc = rms_norm(bqc.reshape(b, lc, B_HEADS, B_HEAD_DIM), b_qnorm[i])
            out_bc = sink_attn_ctx(q_bc, k_bc, v_bc, b_sink[i])
            q_cc = rms_norm(cqc.reshape(b, lc, C_HEADS, 2, C_QK_DIM), c_qnorm[i])
            o_cc = diff_attn_block(q_cc, k_cc, v_cc, lam)
            out_cc = (rms_norm(o_cc, c_subln[i]) * (1 - lam_init)).reshape(b, lc, C_WIDTH)
            ctx = ctx + cg1 * (jnp.concatenate([out_ac, out_bc, out_cc], axis=-1) @ w_out[i])
            ctx = ctx + cg2 * conv_ffn(modulate(rms_norm(ctx, norm2_g[i]), csh2, csc2),
                                       w_up[i], conv_w[i], conv_b[i], w_down[i])
    return x
```

```python
import functools
import math

import jax
import jax.numpy as jnp
import numpy as np
from jax import lax
from jax.experimental import pallas as pl
from jax.experimental.pallas import tpu as pltpu

F32 = jnp.float32
BF16 = jnp.bfloat16

D_MODEL = 1024
BATCH = 16
SEQ = 2048
DEPTH = 4
GRID_W = 64
CTX_LEN = 256
A_WIDTH = 256
A_GROUPS = 4
CHUNK = 128
B_HEAD_DIM = 64
B_HEADS = 8
B_KV_HEADS = 2
WINDOW = 128
C_V_DIM = 64
C_QK_DIM = 32
C_HEADS = 4
PROJ_WIDTH = 2048
MIX_WIDTH = 1024
D_FF = 2816
ROPE_BASE = 10000.0
EPS = 1e-6

LOG2E = 1.4426950408889634
NEG_BIG = -1e30

LANES = 128
BF16_SUBLANES = 16

NLAT = BATCH * SEQ
NCTX = BATCH * CTX_LEN
NTOK = NLAT + NCTX
COND_ROWS = 24
CTX_COND_ROW = BATCH

PROJ_ROWS = 512
ATT_ROWS = 256
FFN_ROWS = 1024
FF_TILE = 256
N_FF_TILES = D_FF // FF_TILE
ADA_TILE = 1536

OFF_AU, OFF_AV, OFF_BQ, OFF_BK, OFF_BV, OFF_CQ, OFF_CK, OFF_CV = (
    0, 256, 512, 1024, 1152, 1280, 1536, 1792)


def _params(sem, vmem_mb):
    return pltpu.CompilerParams(dimension_semantics=sem,
                                vmem_limit_bytes=vmem_mb * 1024 * 1024)


def _mod_kernel(cond_ref, w_ref, b_ref, o_ref):
    c = cond_ref[...]
    s = c / (1.0 + jnp.exp(-c))
    o_ref[0] = jnp.dot(s.astype(BF16), w_ref[0], preferred_element_type=F32) + b_ref[0]


def _modulation_all(cond, w_ada, b_ada):
    n_tiles = (6 * D_MODEL) // ADA_TILE
    return pl.pallas_call(
        _mod_kernel,
        grid=(DEPTH, n_tiles),
        in_specs=[
            pl.BlockSpec((COND_ROWS, D_MODEL), lambda l, j: (0, 0)),
            pl.BlockSpec((1, D_MODEL, ADA_TILE), lambda l, j: (l, 0, j)),
            pl.BlockSpec((1, 1, ADA_TILE), lambda l, j: (l, 0, j)),
        ],
        out_specs=pl.BlockSpec((1, COND_ROWS, ADA_TILE), lambda l, j: (l, 0, j)),
        out_shape=jax.ShapeDtypeStruct((DEPTH, COND_ROWS, 6 * D_MODEL), F32),
        compiler_params=_params(("arbitrary", "arbitrary"), 32),
        name="modulation",
    )(cond, w_ada, b_ada.reshape(DEPTH, 1, 6 * D_MODEL))


def _norm_modulate(x, gain, shift, scale):
    ms = jnp.mean(x * x, axis=-1, keepdims=True)
    return (x * lax.rsqrt(ms + EPS) * gain) * (1.0 + scale) + shift


def _gelu_tanh(x):
    return 0.5 * x * (1.0 + jnp.tanh(math.sqrt(2.0 / math.pi) * (x + 0.044715 * (x * x * x))))


def _group_mean_sq(t, g):
    return jnp.dot((t * t).astype(BF16), g, preferred_element_type=F32)


def _swap_halves(t, half):
    lane = lax.broadcasted_iota(jnp.int32, t.shape, 1)
    first = (lane & (2 * half - 1)) < half
    return jnp.where(first, pltpu.roll(t, LANES - half, 1), pltpu.roll(t, half, 1))


def _lane_iota(shape):
    return lax.broadcasted_iota(jnp.int32, shape, 1)


def _proj_kernel(x_ref, sh_ref, sc_ref, ng_ref, w_ref, g64_ref, g32_ref,
                 cosb_ref, sinb_ref, cosc_ref, sinc_ref, gains_ref,
                 u_ref, vn_ref, qb_ref, kbt_ref, vb_ref, qc_ref, kct_ref, vc_ref):
    h = _norm_modulate(x_ref[...], ng_ref[...], sh_ref[0], sc_ref[0])
    z = jnp.dot(h.astype(BF16), w_ref[...], preferred_element_type=F32)

    u_ref[...] = _gelu_tanh(z[:, OFF_AU:OFF_AU + A_WIDTH])
    v = _gelu_tanh(z[:, OFF_AV:OFF_AV + A_WIDTH])
    vn_ref[...] = (v * lax.rsqrt(_group_mean_sq(v, g64_ref[...]) + EPS)).astype(BF16)

    def norm_rope(t, ms, gain, cos_ref, sin_ref, half):
        tg = t * gain
        return lax.rsqrt(ms + EPS) * (tg * cos_ref[...] + _swap_halves(tg, half) * sin_ref[...])

    for t in range(4):
        lo = OFF_BQ + 256 * (t // 2)
        if t % 2 == 0:
            q2 = z[:, lo:lo + 256]
            ms2 = _group_mean_sq(q2, g64_ref[...])
        s = slice(LANES * (t % 2), LANES * (t % 2) + LANES)
        qb_ref[:, LANES * t:LANES * t + LANES] = norm_rope(
            q2[:, s], ms2[:, s], gains_ref[0:1, :], cosb_ref, sinb_ref, 16).astype(BF16)

    kb = z[:, OFF_BK:OFF_BK + LANES]
    kb = norm_rope(kb, _group_mean_sq(kb, g64_ref[0:LANES, 0:LANES]),
                   gains_ref[1:2, :], cosb_ref, sinb_ref, 16)
    kbt = kb.T.astype(BF16)
    kbt_ref[0:LANES, :] = kbt
    kbt_ref[LANES:LANES + 64, :] = kbt[64:LANES, :]
    kbt_ref[LANES + 64:2 * LANES, :] = kbt[0:64, :]

    vb = z[:, OFF_BV:OFF_BV + LANES]
    ones = jnp.ones((PROJ_ROWS, LANES), BF16)
    vb_ref[:, 0:LANES] = vb.astype(BF16)
    vb_ref[:, LANES:2 * LANES] = ones
    vb_ref[:, 2 * LANES:3 * LANES] = pltpu.roll(vb, 64, 1).astype(BF16)
    vb_ref[:, 3 * LANES:4 * LANES] = ones

    cq = z[:, OFF_CQ:OFF_CQ + 256]
    msq = _group_mean_sq(cq, g32_ref[...])
    ck = z[:, OFF_CK:OFF_CK + 256]
    msk = _group_mean_sq(ck, g32_ref[...])
    for t in range(2):
        s = slice(LANES * t, LANES * t + LANES)
        qc_ref[:, s] = norm_rope(cq[:, s], msq[:, s], gains_ref[2:3, :],
                                 cosc_ref, sinc_ref, 8).astype(BF16)
        kc = norm_rope(ck[:, s], msk[:, s], gains_ref[3:4, :], cosc_ref, sinc_ref, 8)
        kct_ref[s, :] = kc.T.astype(BF16)

    cv = z[:, OFF_CV:OFF_CV + 256]
    vc_ref[:, 0:LANES] = cv[:, 0:LANES].astype(BF16)
    vc_ref[:, LANES:2 * LANES] = ones
    vc_ref[:, 2 * LANES:3 * LANES] = cv[:, LANES:2 * LANES].astype(BF16)
    vc_ref[:, 3 * LANES:4 * LANES] = ones


def _proj(xs, mod, norm_g, w_in, consts, gains):
    n_lat_blocks = NLAT // PROJ_ROWS
    blocks_per_seq = SEQ // PROJ_ROWS

    def cond_row(i):
        return jnp.where(i < n_lat_blocks, i // blocks_per_seq, CTX_COND_ROW)

    def pos_block(i):
        return jnp.where(i < n_lat_blocks, i % blocks_per_seq, blocks_per_seq)

    row = lambda i: (i, 0)
    col = lambda i: (0, i)
    whole = lambda i: (0, 0)
    pos = lambda i: (pos_block(i), 0)
    out_shape = (
        jax.ShapeDtypeStruct((NTOK, 256), F32),
        jax.ShapeDtypeStruct((NTOK, 256), BF16),
        jax.ShapeDtypeStruct((NTOK, 512), BF16),
        jax.ShapeDtypeStruct((256, NTOK), BF16),
        jax.ShapeDtypeStruct((NTOK, 512), BF16),
        jax.ShapeDtypeStruct((NTOK, 256), BF16),
        jax.ShapeDtypeStruct((256, NTOK), BF16),
        jax.ShapeDtypeStruct((NTOK, 512), BF16),
    )
    out_specs = (
        pl.BlockSpec((PROJ_ROWS, 256), row),
        pl.BlockSpec((PROJ_ROWS, 256), row),
        pl.BlockSpec((PROJ_ROWS, 512), row),
        pl.BlockSpec((256, PROJ_ROWS), col),
        pl.BlockSpec((PROJ_ROWS, 512), row),
        pl.BlockSpec((PROJ_ROWS, 256), row),
        pl.BlockSpec((256, PROJ_ROWS), col),
        pl.BlockSpec((PROJ_ROWS, 512), row),
    )
    return pl.pallas_call(
        _proj_kernel,
        grid=(NTOK // PROJ_ROWS,),
        in_specs=[
            pl.BlockSpec((PROJ_ROWS, D_MODEL), row),
            pl.BlockSpec((1, 1, D_MODEL), lambda i: (cond_row(i) * 6 + 0, 0, 0)),
            pl.BlockSpec((1, 1, D_MODEL), lambda i: (cond_row(i) * 6 + 1, 0, 0)),
            pl.BlockSpec((1, D_MODEL), whole),
            pl.BlockSpec((D_MODEL, PROJ_WIDTH), whole),
            pl.BlockSpec((256, 256), whole),
            pl.BlockSpec((256, 256), whole),
            pl.BlockSpec((PROJ_ROWS, LANES), pos),
            pl.BlockSpec((PROJ_ROWS, LANES), pos),
            pl.BlockSpec((PROJ_ROWS, LANES), pos),
            pl.BlockSpec((PROJ_ROWS, LANES), pos),
            pl.BlockSpec((8, LANES), whole),
        ],
        out_specs=out_specs,
        out_shape=out_shape,
        compiler_params=_params(("arbitrary",), 48),
        name="proj",
    )(xs, mod, mod, norm_g, w_in, consts["g64"], consts["g32"],
      consts["cosb"], consts["sinb"], consts["cosc"], consts["sinc"], gains)


def _exp2_bf16(s, m):
    return jnp.exp2(s - m).astype(BF16)


def _row_max(s):
    return jnp.max(s, axis=-1, keepdims=True)


def _mixer_kernel(u_ref, vn_ref, qb_ref, qc_ref,
                  kbl_ref, kbc_ref, vbl_ref, vbc_ref,
                  kcl_ref, kcc_ref, vcl_ref, vcc_ref,
                  sink_ref, lam_ref, wcat_ref, abias_ref, subln_ref,
                  mix_ref, *, lam_init, n_lat_qblocks):
    n = pl.program_id(1)
    lane = _lane_iota((ATT_ROWS, LANES))
    low_half = lane < 64
    dot = functools.partial(jnp.dot, preferred_element_type=F32)

    for c in range(ATT_ROWS // CHUNK):
        rows = slice(CHUNK * c, CHUNK * c + CHUNK)
        for t in range(2):
            cols = slice(LANES * t, LANES * t + LANES)
            vt = vn_ref[rows, cols].astype(F32)
            keep_lo = _lane_iota(vt.shape) < 64
            rhs = jnp.concatenate([jnp.where(keep_lo, vt, 0.0), jnp.where(keep_lo, 0.0, vt)],
                                  axis=0).astype(BF16)
            mixed = dot(wcat_ref[t], rhs) + abias_ref[:, cols]
            mix_ref[rows, cols] = (u_ref[rows, cols] * mixed).astype(BF16)

    lp = lam_ref[...]
    lam = (jnp.exp(jnp.sum(lp[0:1] * lp[1:2], axis=-1, keepdims=True))
           - jnp.exp(jnp.sum(lp[2:3] * lp[3:4], axis=-1, keepdims=True)) + lam_init)

    def head_mask(offset, width):
        return jnp.where((lane >= offset) & (lane < offset + width), 1.0, 0.0)

    def softmax_pv(scores, values, sink2):
        m = _row_max(scores[0])
        for s in scores[1:]:
            m = jnp.maximum(m, _row_max(s))
        if sink2 is not None:
            m = jnp.maximum(m, sink2)
        oa = dot(_exp2_bf16(scores[0], m), values[0])
        for s, v in zip(scores[1:], values[1:]):
            oa = oa + dot(_exp2_bf16(s, m), v)
        den = oa[:, LANES:2 * LANES]
        if sink2 is not None:
            den = den + jnp.exp2(sink2 - m)
        return oa[:, 0:LANES] / den

    def mixer_b(latent):
        if latent:
            start = pl.multiple_of(jnp.clip(n * ATT_ROWS - WINDOW, 0, SEQ - 2 * ATT_ROWS), LANES)
            qpos = n * ATT_ROWS + lax.broadcasted_iota(jnp.int32, (ATT_ROWS, 2 * ATT_ROWS), 0)
            kpos = start + lax.broadcasted_iota(jnp.int32, (ATT_ROWS, 2 * ATT_ROWS), 1)
            valid = jnp.abs(qpos - kpos) <= WINDOW
        for p in range(B_HEADS // 2):
            qt = qb_ref[:, LANES * p:LANES * p + LANES].astype(F32)
            outs = []
            for r in range(2):
                h = 2 * p + r
                j = h // (B_HEADS // B_KV_HEADS)
                var = 0 if j == r else 1
                krows = slice(LANES * var, LANES * var + LANES)
                vcols = slice(256 * var, 256 * var + 256)
                qm = (qt * head_mask(64 * r, 64)).astype(BF16)
                sink2 = sink_ref[h] * LOG2E
                s_ctx = dot(qm, kbc_ref[krows, :])
                if latent:
                    s_win = dot(qm, kbl_ref[krows, pl.ds(start, 2 * ATT_ROWS)])
                    s_win = jnp.where(valid, s_win, NEG_BIG)
                    outs.append(softmax_pv([s_win, s_ctx],
                                           [vbl_ref[pl.ds(start, 2 * ATT_ROWS), vcols],
                                            vbc_ref[:, vcols]], sink2))
                else:
                    outs.append(softmax_pv([s_ctx], [vbc_ref[:, vcols]], sink2))
            mix_ref[:, A_WIDTH + LANES * p:A_WIDTH + LANES * p + LANES] = (
                jnp.where(low_half, outs[0], outs[1]).astype(BF16))

    def mixer_c(latent):
        for t in range(C_HEADS // 2):
            qt = qc_ref[:, LANES * t:LANES * t + LANES].astype(F32)
            krows = slice(LANES * t, LANES * t + LANES)
            vcols = slice(256 * t, 256 * t + 256)
            outs = []
            for r in range(2):
                maps = []
                for mp in range(2):
                    qm = (qt * head_mask(64 * r + C_QK_DIM * mp, C_QK_DIM)).astype(BF16)
                    s_ctx = dot(qm, kcc_ref[krows, :])
                    if latent:
                        s_lat = dot(qm, kcl_ref[krows, :])
                        maps.append(softmax_pv([s_lat, s_ctx],
                                               [vcl_ref[:, vcols], vcc_ref[:, vcols]], None))
                    else:
                        maps.append(softmax_pv([s_ctx], [vcc_ref[:, vcols]], None))
                outs.append(maps[0] - lam * maps[1])
            o = jnp.where(low_half, outs[0], outs[1])
            o2 = o * o
            ms_lo = jnp.sum(jnp.where(low_half, o2, 0.0), axis=-1, keepdims=True) * (1.0 / C_V_DIM)
            ms_hi = jnp.sum(jnp.where(low_half, 0.0, o2), axis=-1, keepdims=True) * (1.0 / C_V_DIM)
            rs = jnp.where(low_half, lax.rsqrt(ms_lo + EPS), lax.rsqrt(ms_hi + EPS))
            off = A_WIDTH + 512 + LANES * t
            mix_ref[:, off:off + LANES] = (o * rs * subln_ref[...]).astype(BF16)

    @pl.when(n < n_lat_qblocks)
    def _():
        mixer_b(True)
        mixer_c(True)

    @pl.when(n >= n_lat_qblocks)
    def _():
        mixer_b(False)
        mixer_c(False)


def _mixers(u, vn, qb, kbt, vb, qc, kct, vc, sink, c_lam, wcat, abias, subln, lam_init, with_ctx):
    n_lat_qblocks = SEQ // ATT_ROWS
    n_qblocks = n_lat_qblocks + (1 if with_ctx else 0)
    lat_q_total = NLAT // ATT_ROWS
    out_rows = NTOK if with_ctx else NLAT

    def qrow(b, n):
        return (jnp.where(n < n_lat_qblocks, b * n_lat_qblocks + n, lat_q_total + b), 0)

    lat_rows = lambda b, n: (b, 0)
    ctx_rows = lambda b, n: (NLAT // CTX_LEN + b, 0)
    lat_cols = lambda b, n: (0, b)
    ctx_cols = lambda b, n: (0, NLAT // CTX_LEN + b)
    whole2 = lambda b, n: (0, 0)
    kernel = functools.partial(_mixer_kernel, lam_init=lam_init, n_lat_qblocks=n_lat_qblocks)
    return pl.pallas_call(
        kernel,
        grid=(BATCH, n_qblocks),
        in_specs=[
            pl.BlockSpec((ATT_ROWS, 256), qrow),
            pl.BlockSpec((ATT_ROWS, 256), qrow),
            pl.BlockSpec((ATT_ROWS, 512), qrow),
            pl.BlockSpec((ATT_ROWS, 256), qrow),
            pl.BlockSpec((256, SEQ), lat_cols),
            pl.BlockSpec((256, CTX_LEN), ctx_cols),
            pl.BlockSpec((SEQ, 512), lat_rows),
            pl.BlockSpec((CTX_LEN, 512), ctx_rows),
            pl.BlockSpec((256, SEQ), lat_cols),
            pl.BlockSpec((256, CTX_LEN), ctx_cols),
            pl.BlockSpec((SEQ, 512), lat_rows),
            pl.BlockSpec((CTX_LEN, 512), ctx_rows),
            pl.BlockSpec(memory_space=pltpu.SMEM),
            pl.BlockSpec((4, C_QK_DIM), whole2),
            pl.BlockSpec((2, CHUNK, 256), lambda b, n: (0, 0, 0)),
            pl.BlockSpec((CHUNK, 256), whole2),
            pl.BlockSpec((1, LANES), whole2),
        ],
        out_specs=pl.BlockSpec((ATT_ROWS, MIX_WIDTH), qrow),
        out_shape=jax.ShapeDtypeStruct((out_rows, MIX_WIDTH), BF16),
        compiler_params=_params(("arbitrary", "arbitrary"), 48),
        name="mixers",
    )(u, vn, qb, qc, kbt, kbt, vb, vb, kct, kct, vc, vc, sink, c_lam, wcat, abias, subln)


def _outproj_kernel(mix_ref, x_ref, g1_ref, sh_ref, sc_ref, ng_ref, w_ref, x1_ref, h2_ref):
    y = jnp.dot(mix_ref[...], w_ref[...], preferred_element_type=F32)
    x1 = x_ref[...] + g1_ref[0] * y
    x1_ref[...] = x1
    h2_ref[...] = _norm_modulate(x1, ng_ref[...], sh_ref[0], sc_ref[0]).astype(BF16)


def _outproj(mix, xs, mod, norm_g, w_out, n_rows):
    n_lat_blocks = NLAT // PROJ_ROWS
    blocks_per_seq = SEQ // PROJ_ROWS

    def cond_row(i):
        return jnp.where(i < n_lat_blocks, i // blocks_per_seq, CTX_COND_ROW)

    row = lambda i: (i, 0)
    whole = lambda i: (0, 0)
    modspec = lambda k: pl.BlockSpec((1, 1, D_MODEL), lambda i: (cond_row(i) * 6 + k, 0, 0))
    return pl.pallas_call(
        _outproj_kernel,
        grid=(n_rows // PROJ_ROWS,),
        in_specs=[
            pl.BlockSpec((PROJ_ROWS, MIX_WIDTH), row),
            pl.BlockSpec((PROJ_ROWS, D_MODEL), row),
            modspec(2), modspec(3), modspec(4),
            pl.BlockSpec((1, D_MODEL), whole),
            pl.BlockSpec((MIX_WIDTH, D_MODEL), whole),
        ],
        out_specs=(pl.BlockSpec((PROJ_ROWS, D_MODEL), row),
                   pl.BlockSpec((PROJ_ROWS, D_MODEL), row)),
        out_shape=(jax.ShapeDtypeStruct((n_rows, D_MODEL), F32),
                   jax.ShapeDtypeStruct((n_rows, D_MODEL), BF16)),
        compiler_params=_params(("arbitrary",), 40),
        name="outproj",
    )(mix, xs, mod, mod, mod, norm_g, w_out)


HALO = BF16_SUBLANES


def _ffn_kernel(h_ref, hprev_ref, hnext_ref, x1_ref, g2_ref, mask_ref,
                wg_ref, wv_ref, cwg_ref, cwv_ref, cbg_ref, cbv_ref, wd_ref,
                o_ref, lhs_ref, acc_ref):
    j = pl.program_id(1)

    @pl.when(j == 0)
    def _():
        lhs_ref[0:HALO, :] = hprev_ref[...]
        lhs_ref[HALO:HALO + FFN_ROWS, :] = h_ref[...]
        lhs_ref[HALO + FFN_ROWS:2 * HALO + FFN_ROWS, :] = hnext_ref[...]
        acc_ref[...] = jnp.zeros_like(acc_ref)

    lhs = lhs_ref[...]
    mprev = mask_ref[0, 0]
    mnext = mask_ref[0, 1]

    def conv(w_ref, cw_ref, cb_ref):
        a = jnp.dot(lhs, w_ref[...], preferred_element_type=F32)
        prev = a[HALO - 1:HALO - 1 + FFN_ROWS, :]
        cur = a[HALO:HALO + FFN_ROWS, :]
        nxt = a[HALO + 1:HALO + 1 + FFN_ROWS, :]
        return (prev * mprev * cw_ref[0:1, :] + cur * cw_ref[1:2, :]
                + nxt * mnext * cw_ref[2:3, :] + cb_ref[...])

    gate = conv(wg_ref, cwg_ref, cbg_ref)
    val = conv(wv_ref, cwv_ref, cbv_ref)
    y = (gate / (1.0 + jnp.exp(-gate))) * val
    acc_ref[...] += jnp.dot(y.astype(BF16), wd_ref[...], preferred_element_type=F32)

    @pl.when(j == N_FF_TILES - 1)
    def _():
        o_ref[...] = x1_ref[...] + g2_ref[0] * acc_ref[...]


def _ffn(h2, x1, mod, masks, w_up, conv_w, conv_b, w_down, n_rows):
    n_blocks = n_rows // FFN_ROWS
    n_lat_blocks = NLAT // FFN_ROWS
    blocks_per_seq = SEQ // FFN_ROWS
    halo_per_block = FFN_ROWS // HALO
    last_halo = n_rows // HALO - 1

    def cond_row(i):
        return jnp.where(i < n_lat_blocks, i // blocks_per_seq, CTX_COND_ROW)

    def mask_kind(i):
        return jnp.where(i < n_lat_blocks, i % blocks_per_seq, blocks_per_seq)

    row = lambda i, j: (i, 0)
    return pl.pallas_call(
        _ffn_kernel,
        grid=(n_blocks, N_FF_TILES),
        in_specs=[
            pl.BlockSpec((FFN_ROWS, D_MODEL), row),
            pl.BlockSpec((HALO, D_MODEL), lambda i, j: (jnp.maximum(i * halo_per_block - 1, 0), 0)),
            pl.BlockSpec((HALO, D_MODEL),
                         lambda i, j: (jnp.minimum((i + 1) * halo_per_block, last_halo), 0)),
            pl.BlockSpec((FFN_ROWS, D_MODEL), row),
            pl.BlockSpec((1, 1, D_MODEL), lambda i, j: (cond_row(i) * 6 + 5, 0, 0)),
            pl.BlockSpec((1, 2, FFN_ROWS, FF_TILE), lambda i, j: (mask_kind(i), 0, 0, 0)),
            pl.BlockSpec((D_MODEL, FF_TILE), lambda i, j: (0, j)),
            pl.BlockSpec((D_MODEL, FF_TILE), lambda i, j: (0, N_FF_TILES + j)),
            pl.BlockSpec((3, FF_TILE), lambda i, j: (0, j)),
            pl.BlockSpec((3, FF_TILE), lambda i, j: (0, N_FF_TILES + j)),
            pl.BlockSpec((1, FF_TILE), lambda i, j: (0, j)),
            pl.BlockSpec((1, FF_TILE), lambda i, j: (0, N_FF_TILES + j)),
            pl.BlockSpec((FF_TILE, D_MODEL), lambda i, j: (j, 0)),
        ],
        out_specs=pl.BlockSpec((FFN_ROWS, D_MODEL), row),
        out_shape=jax.ShapeDtypeStruct((n_rows, D_MODEL), F32),
        scratch_shapes=[pltpu.VMEM((FFN_ROWS + 2 * HALO, D_MODEL), BF16),
                        pltpu.VMEM((FFN_ROWS, D_MODEL), F32)],
        compiler_params=_params(("arbitrary", "arbitrary"), 56),
        name="ffn",
    )(h2, h2, h2, x1, mod, masks, w_up, w_up, conv_w, conv_w, conv_b, conv_b, w_down)


def _rope_tables(head_dim):
    n_freq = head_dim // 4
    rows = SEQ // GRID_W
    row = jnp.repeat(jnp.arange(rows, dtype=F32), GRID_W)
    col = jnp.tile(jnp.arange(GRID_W, dtype=F32), rows)
    inv_freq = ROPE_BASE ** (-jnp.arange(n_freq, dtype=F32) / n_freq)
    ang = jnp.stack([row[:, None] * inv_freq, col[:, None] * inv_freq], axis=1)
    lane = np.arange(LANES)
    within = lane % head_dim
    axis = within // (2 * n_freq)
    second = (within % (2 * n_freq)) >= n_freq
    ang = ang[:, axis, lane % n_freq]
    cos = jnp.cos(ang)
    sin = jnp.where(jnp.asarray(second)[None, :], jnp.sin(ang), -jnp.sin(ang))
    cos = jnp.concatenate([cos, jnp.ones((PROJ_ROWS, LANES), F32)], axis=0)
    sin = jnp.concatenate([sin, jnp.zeros((PROJ_ROWS, LANES), F32)], axis=0)
    return cos, sin


def _group_matrix(width):
    idx = np.arange(256)
    g = (idx[:, None] // width == idx[None, :] // width).astype(np.float32) / width
    return jnp.asarray(g, dtype=BF16)


def _conv_masks():
    r = np.arange(FFN_ROWS)
    kinds = []
    for k in range(SEQ // FFN_ROWS):
        pos = k * FFN_ROWS + r
        kinds.append((pos != 0, pos != SEQ - 1))
    kinds.append((r % CTX_LEN != 0, r % CTX_LEN != CTX_LEN - 1))
    m = np.stack([np.stack([np.broadcast_to(a[:, None], (FFN_ROWS, FF_TILE)),
                            np.broadcast_to(b[:, None], (FFN_ROWS, FF_TILE))]) for a, b in kinds])
    return jnp.asarray(m.astype(np.float32))


def _constants():
    cosb, sinb = _rope_tables(B_HEAD_DIM)
    cosc, sinc = _rope_tables(C_QK_DIM)
    return dict(cosb=cosb, sinb=sinb, cosc=cosc, sinc=sinc,
                g64=_group_matrix(64), g32=_group_matrix(32), masks=_conv_masks())


def kernel(x, c, ctx, c_ctx, w_ada, b_ada, norm1_g, norm2_g, w_in, a_ws, a_bs, b_qnorm, b_knorm,
           b_sink, c_qnorm, c_knorm, c_lam, c_subln, w_out, w_up, conv_w, conv_b, w_down):
    consts = _constants()
    xs = jnp.concatenate([x.reshape(NLAT, D_MODEL), ctx.reshape(NCTX, D_MODEL)], axis=0)

    cond = jnp.zeros((COND_ROWS, D_MODEL), F32).at[:BATCH].set(c).at[CTX_COND_ROW].set(c_ctx)
    mod_all = _modulation_all(cond, w_ada.astype(BF16), b_ada)
    mod_all = mod_all.reshape(DEPTH, COND_ROWS * 6, 1, D_MODEL)

    w_in_b = w_in.astype(BF16)
    w_out_b = w_out.astype(BF16)
    w_up_b = w_up.astype(BF16)
    w_down_b = w_down.astype(BF16)

    for i in range(DEPTH):
        last = i == DEPTH - 1
        lam_init = 0.8 - 0.6 * math.exp(-0.3 * i)
        mod = mod_all[i]
        zeros = jnp.zeros((4, LANES), F32)
        gains = jnp.concatenate([
            jnp.tile(b_qnorm[i] * (B_HEAD_DIM ** -0.5 * LOG2E), LANES // B_HEAD_DIM)[None],
            jnp.tile(b_knorm[i], LANES // B_HEAD_DIM)[None],
            jnp.tile(c_qnorm[i] * (C_QK_DIM ** -0.5 * LOG2E), LANES // C_QK_DIM)[None],
            jnp.tile(c_knorm[i], LANES // C_QK_DIM)[None],
            zeros], axis=0)
        u, vn, qb, kbt, vb, qc, kct, vc = _proj(
            xs, mod, norm1_g[i][None], w_in_b[i], consts, gains)

        wcat = jnp.concatenate([a_ws[i][0::2], a_ws[i][1::2]], axis=-1).astype(BF16)
        abias = jnp.repeat(a_bs[i].T, A_WIDTH // A_GROUPS, axis=1)
        subln = jnp.tile(c_subln[i] * (1.0 - lam_init), LANES // C_V_DIM)[None]
        mix = _mixers(u, vn, qb, kbt, vb, qc, kct, vc, b_sink[i], c_lam[i], wcat, abias, subln,
                      lam_init, with_ctx=not last)

        n_rows = NLAT if last else NTOK
        x1, h2 = _outproj(mix, xs, mod, norm2_g[i][None], w_out_b[i], n_rows)
        xs = _ffn(h2, x1, mod, consts["masks"], w_up_b[i], conv_w[i].reshape(3, 2 * D_FF),
                  conv_b[i][None], w_down_b[i], n_rows)

    return xs[:NLAT].reshape(BATCH, SEQ, D_MODEL)
```

```python
import functools
import math

import jax
import jax.numpy as jnp
import numpy as np
from jax import lax
from jax.experimental import pallas as pl
from jax.experimental.pallas import tpu as pltpu

F32 = jnp.float32
BF16 = jnp.bfloat16

D_MODEL = 1024
BATCH = 16
SEQ = 2048
DEPTH = 4
GRID_W = 64
CTX_LEN = 256
A_WIDTH = 256
A_GROUPS = 4
CHUNK = 128
B_HEAD_DIM = 64
B_HEADS = 8
B_KV_HEADS = 2
WINDOW = 128
C_V_DIM = 64
C_QK_DIM = 32
C_HEADS = 4
PROJ_WIDTH = 2048
MIX_WIDTH = 1024
D_FF = 2816
ROPE_BASE = 10000.0
EPS = 1e-6

LOG2E = 1.4426950408889634
NEG_BIG = -1e30

LANES = 128
BF16_SUBLANES = 16

NLAT = BATCH * SEQ
NCTX = BATCH * CTX_LEN
NTOK = NLAT + NCTX
COND_ROWS = 24
CTX_COND_ROW = BATCH

PROJ_ROWS = 512
ATT_ROWS = 256
FFN_ROWS = 1024
FF_TILE = 256
N_FF_TILES = D_FF // FF_TILE
ADA_TILE = 1536

OFF_AU, OFF_AV, OFF_BQ, OFF_BK, OFF_BV, OFF_CQ, OFF_CK, OFF_CV = (
    0, 256, 512, 1024, 1152, 1280, 1536, 1792)


def _params(sem, vmem_mb):
    return pltpu.CompilerParams(dimension_semantics=sem,
                                vmem_limit_bytes=vmem_mb * 1024 * 1024)


def _mod_kernel(cond_ref, w_ref, b_ref, o_ref):
    c = cond_ref[...]
    s = c / (1.0 + jnp.exp(-c))
    o_ref[0] = jnp.dot(s.astype(BF16), w_ref[0], preferred_element_type=F32) + b_ref[0]


def _modulation_all(cond, w_ada, b_ada):
    n_tiles = (6 * D_MODEL) // ADA_TILE
    return pl.pallas_call(
        _mod_kernel,
        grid=(DEPTH, n_tiles),
        in_specs=[
            pl.BlockSpec((COND_ROWS, D_MODEL), lambda l, j: (0, 0)),
            pl.BlockSpec((1, D_MODEL, ADA_TILE), lambda l, j: (l, 0, j)),
            pl.BlockSpec((1, 1, ADA_TILE), lambda l, j: (l, 0, j)),
        ],
        out_specs=pl.BlockSpec((1, COND_ROWS, ADA_TILE), lambda l, j: (l, 0, j)),
        out_shape=jax.ShapeDtypeStruct((DEPTH, COND_ROWS, 6 * D_MODEL), F32),
        compiler_params=_params(("arbitrary", "arbitrary"), 32),
        name="modulation",
    )(cond, w_ada, b_ada.reshape(DEPTH, 1, 6 * D_MODEL))


def _norm_modulate(x, gain, shift, scale):
    ms = jnp.mean(x * x, axis=-1, keepdims=True)
    return (x * lax.rsqrt(ms + EPS) * gain) * (1.0 + scale) + shift


def _gelu_tanh(x):
    return 0.5 * x * (1.0 + jnp.tanh(math.sqrt(2.0 / math.pi) * (x + 0.044715 * (x * x * x))))


def _group_mean_sq(t, g):
    return jnp.dot((t * t).astype(BF16), g, preferred_element_type=F32)


def _swap_halves(t, half):
    lane = lax.broadcasted_iota(jnp.int32, t.shape, 1)
    first = (lane & (2 * half - 1)) < half
    return jnp.where(first, pltpu.roll(t, LANES - half, 1), pltpu.roll(t, half, 1))


def _lane_iota(shape):
    return lax.broadcasted_iota(jnp.int32, shape, 1)


def _proj_kernel(x_ref, sh_ref, sc_ref, ng_ref, w_ref, g64_ref, g32_ref,
                 cosb_ref, sinb_ref, cosc_ref, sinc_ref, gains_ref,
                 u_ref, vn_ref, qb_ref, kbt_ref, vb_ref, qc_ref, kct_ref, vc_ref):
    h = _norm_modulate(x_ref[...], ng_ref[...], sh_ref[0], sc_ref[0])
    z = jnp.dot(h.astype(BF16), w_ref[...], preferred_element_type=F32)

    u_ref[...] = _gelu_tanh(z[:, OFF_AU:OFF_AU + A_WIDTH])
    v = _gelu_tanh(z[:, OFF_AV:OFF_AV + A_WIDTH])
    vn_ref[...] = (v * lax.rsqrt(_group_mean_sq(v, g64_ref[...]) + EPS)).astype(BF16)

    def norm_rope(t, ms, gain, cos_ref, sin_ref, half):
        tg = t * gain
        return lax.rsqrt(ms + EPS) * (tg * cos_ref[...] + _swap_halves(tg, half) * sin_ref[...])

    for t in range(4):
        lo = OFF_BQ + 256 * (t // 2)
        if t % 2 == 0:
            q2 = z[:, lo:lo + 256]
            ms2 = _group_mean_sq(q2, g64_ref[...])
        s = slice(LANES * (t % 2), LANES * (t % 2) + LANES)
        qb_ref[:, LANES * t:LANES * t + LANES] = norm_rope(
            q2[:, s], ms2[:, s], gains_ref[0:1, :], cosb_ref, sinb_ref, 16).astype(BF16)

    kb = z[:, OFF_BK:OFF_BK + LANES]
    kb = norm_rope(kb, _group_mean_sq(kb, g64_ref[0:LANES, 0:LANES]),
                   gains_ref[1:2, :], cosb_ref, sinb_ref, 16)
    kbt = kb.T.astype(BF16)
    kbt_ref[0:LANES, :] = kbt
    kbt_ref[LANES:LANES + 64, :] = kbt[64:LANES, :]
    kbt_ref[LANES + 64:2 * LANES, :] = kbt[0:64, :]

    vb = z[:, OFF_BV:OFF_BV + LANES]
    ones = jnp.ones((PROJ_ROWS, LANES), BF16)
    vb_ref[:, 0:LANES] = vb.astype(BF16)
    vb_ref[:, LANES:2 * LANES] = ones
    vb_ref[:, 2 * LANES:3 * LANES] = pltpu.roll(vb, 64, 1).astype(BF16)
    vb_ref[:, 3 * LANES:4 * LANES] = ones

    cq = z[:, OFF_CQ:OFF_CQ + 256]
    msq = _group_mean_sq(cq, g32_ref[...])
    ck = z[:, OFF_CK:OFF_CK + 256]
    msk = _group_mean_sq(ck, g32_ref[...])
    for t in range(2):
        s = slice(LANES * t, LANES * t + LANES)
        qc_ref[:, s] = norm_rope(cq[:, s], msq[:, s], gains_ref[2:3, :],
                                 cosc_ref, sinc_ref, 8).astype(BF16)
        kc = norm_rope(ck[:, s], msk[:, s], gains_ref[3:4, :], cosc_ref, sinc_ref, 8)
        kct_ref[s, :] = kc.T.astype(BF16)

    cv = z[:, OFF_CV:OFF_CV + 256]
    vc_ref[:, 0:LANES] = cv[:, 0:LANES].astype(BF16)
    vc_ref[:, LANES:2 * LANES] = ones
    vc_ref[:, 2 * LANES:3 * LANES] = cv[:, LANES:2 * LANES].astype(BF16)
    vc_ref[:, 3 * LANES:4 * LANES] = ones


def _proj(xs, mod, norm_g, w_in, consts, gains):
    n_lat_blocks = NLAT // PROJ_ROWS
    blocks_per_seq = SEQ // PROJ_ROWS

    def cond_row(i):
        return jnp.where(i < n_lat_blocks, i // blocks_per_seq, CTX_COND_ROW)

    def pos_block(i):
        return jnp.where(i < n_lat_blocks, i % blocks_per_seq, blocks_per_seq)

    row = lambda i: (i, 0)
    col = lambda i: (0, i)
    whole = lambda i: (0, 0)
    pos = lambda i: (pos_block(i), 0)
    out_shape = (
        jax.ShapeDtypeStruct((NTOK, 256), F32),
        jax.ShapeDtypeStruct((NTOK, 256), BF16),
        jax.ShapeDtypeStruct((NTOK, 512), BF16),
        jax.ShapeDtypeStruct((256, NTOK), BF16),
        jax.ShapeDtypeStruct((NTOK, 512), BF16),
        jax.ShapeDtypeStruct((NTOK, 256), BF16),
        jax.ShapeDtypeStruct((256, NTOK), BF16),
        jax.ShapeDtypeStruct((NTOK, 512), BF16),
    )
    out_specs = (
        pl.BlockSpec((PROJ_ROWS, 256), row),
        pl.BlockSpec((PROJ_ROWS, 256), row),
        pl.BlockSpec((PROJ_ROWS, 512), row),
        pl.BlockSpec((256, PROJ_ROWS), col),
        pl.BlockSpec((PROJ_ROWS, 512), row),
        pl.BlockSpec((PROJ_ROWS, 256), row),
        pl.BlockSpec((256, PROJ_ROWS), col),
        pl.BlockSpec((PROJ_ROWS, 512), row),
    )
    return pl.pallas_call(
        _proj_kernel,
        grid=(NTOK // PROJ_ROWS,),
        in_specs=[
            pl.BlockSpec((PROJ_ROWS, D_MODEL), row),
            pl.BlockSpec((1, 1, D_MODEL), lambda i: (cond_row(i) * 6 + 0, 0, 0)),
            pl.BlockSpec((1, 1, D_MODEL), lambda i: (cond_row(i) * 6 + 1, 0, 0)),
            pl.BlockSpec((1, D_MODEL), whole),
            pl.BlockSpec((D_MODEL, PROJ_WIDTH), whole),
            pl.BlockSpec((256, 256), whole),
            pl.BlockSpec((256, 256), whole),
            pl.BlockSpec((PROJ_ROWS, LANES), pos),
            pl.BlockSpec((PROJ_ROWS, LANES), pos),
            pl.BlockSpec((PROJ_ROWS, LANES), pos),
            pl.BlockSpec((PROJ_ROWS, LANES), pos),
            pl.BlockSpec((8, LANES), whole),
        ],
        out_specs=out_specs,
        out_shape=out_shape,
        compiler_params=_params(("arbitrary",), 48),
        name="proj",
    )(xs, mod, mod, norm_g, w_in, consts["g64"], consts["g32"],
      consts["cosb"], consts["sinb"], consts["cosc"], consts["sinc"], gains)


def _exp2_bf16(s, m):
    return jnp.exp2(s - m).astype(BF16)


def _row_max(s):
    return jnp.max(s, axis=-1, keepdims=True)


def _mixer_kernel(u_ref, vn_ref, qb_ref, qc_ref,
                  kbl_ref, kbc_ref, vbl_ref, vbc_ref,
                  kcl_ref, kcc_ref, vcl_ref, vcc_ref,
                  sink_ref, lam_ref, wcat_ref, abias_ref, subln_ref,
                  mix_ref, *, lam_init, n_lat_qblocks):
    n = pl.program_id(1)
    lane = _lane_iota((ATT_ROWS, LANES))
    low_half = lane < 64
    dot = functools.partial(jnp.dot, preferred_element_type=F32)

    for c in range(ATT_ROWS // CHUNK):
        rows = slice(CHUNK * c, CHUNK * c + CHUNK)
        for t in range(2):
            cols = slice(LANES * t, LANES * t + LANES)
            vt = vn_ref[rows, cols].astype(F32)
            keep_lo = _lane_iota(vt.shape) < 64
            rhs = jnp.concatenate([jnp.where(keep_lo, vt, 0.0), jnp.where(keep_lo, 0.0, vt)],
                                  axis=0).astype(BF16)
            mixed = dot(wcat_ref[t], rhs) + abias_ref[:, cols]
            mix_ref[rows, cols] = (u_ref[rows, cols] * mixed).astype(BF16)

    lp = lam_ref[...]
    lam = (jnp.exp(jnp.sum(lp[0:1] * lp[1:2], axis=-1, keepdims=True))
           - jnp.exp(jnp.sum(lp[2:3] * lp[3:4], axis=-1, keepdims=True)) + lam_init)

    def head_mask(offset, width):
        return jnp.where((lane >= offset) & (lane < offset + width), 1.0, 0.0)

    def softmax_pv(scores, values, sink2):
        m = _row_max(scores[0])
        for s in scores[1:]:
            m = jnp.maximum(m, _row_max(s))
        if sink2 is not None:
            m = jnp.maximum(m, sink2)
        oa = dot(_exp2_bf16(scores[0], m), values[0])
        for s, v in zip(scores[1:], values[1:]):
            oa = oa + dot(_exp2_bf16(s, m), v)
        den = oa[:, LANES:2 * LANES]
        if sink2 is not None:
            den = den + jnp.exp2(sink2 - m)
        return oa[:, 0:LANES] / den

    def mixer_b(latent):
        if latent:
            start = pl.multiple_of(jnp.clip(n * ATT_ROWS - WINDOW, 0, SEQ - 2 * ATT_ROWS), LANES)
            qpos = n * ATT_ROWS + lax.broadcasted_iota(jnp.int32, (ATT_ROWS, 2 * ATT_ROWS), 0)
            kpos = start + lax.broadcasted_iota(jnp.int32, (ATT_ROWS, 2 * ATT_ROWS), 1)
            valid = jnp.abs(qpos - kpos) <= WINDOW
        for p in range(B_HEADS // 2):
            qt = qb_ref[:, LANES * p:LANES * p + LANES].astype(F32)
            outs = []
            for r in range(2):
                h = 2 * p + r
                j = h // (B_HEADS // B_KV_HEADS)
                var = 0 if j == r else 1
                krows = slice(LANES * var, LANES * var + LANES)
                vcols = slice(256 * var, 256 * var + 256)
                qm = (qt * head_mask(64 * r, 64)).astype(BF16)
                sink2 = sink_ref[h] * LOG2E
                s_ctx = dot(qm, kbc_ref[krows, :])
                if latent:
                    s_win = dot(qm, kbl_ref[krows, pl.ds(start, 2 * ATT_ROWS)])
                    s_win = jnp.where(valid, s_win, NEG_BIG)
                    outs.append(softmax_pv([s_win, s_ctx],
                                           [vbl_ref[pl.ds(start, 2 * ATT_ROWS), vcols],
                                            vbc_ref[:, vcols]], sink2))
                else:
                    outs.append(softmax_pv([s_ctx], [vbc_ref[:, vcols]], sink2))
            mix_ref[:, A_WIDTH + LANES * p:A_WIDTH + LANES * p + LANES] = (
                jnp.where(low_half, outs[0], outs[1]).astype(BF16))

    def mixer_c(latent):
        for t in range(C_HEADS // 2):
            qt = qc_ref[:, LANES * t:LANES * t + LANES].astype(F32)
            krows = slice(LANES * t, LANES * t + LANES)
            vcols = slice(256 * t, 256 * t + 256)
            outs = []
            for r in range(2):
                maps = []
                for mp in range(2):
                    qm = (qt * head_mask(64 * r + C_QK_DIM * mp, C_QK_DIM)).astype(BF16)
                    s_ctx = dot(qm, kcc_ref[krows, :])
                    if latent:
                        s_lat = dot(qm, kcl_ref[krows, :])
                        maps.append(softmax_pv([s_lat, s_ctx],
                                               [vcl_ref[:, vcols], vcc_ref[:, vcols]], None))
                    else:
                        maps.append(softmax_pv([s_ctx], [vcc_ref[:, vcols]], None))
                outs.append(maps[0] - lam * maps[1])
            o = jnp.where(low_half, outs[0], outs[1])
            o2 = o * o
            ms_lo = jnp.sum(jnp.where(low_half, o2, 0.0), axis=-1, keepdims=True) * (1.0 / C_V_DIM)
            ms_hi = jnp.sum(jnp.where(low_half, 0.0, o2), axis=-1, keepdims=True) * (1.0 / C_V_DIM)
            rs = jnp.where(low_half, lax.rsqrt(ms_lo + EPS), lax.rsqrt(ms_hi + EPS))
            off = A_WIDTH + 512 + LANES * t
            mix_ref[:, off:off + LANES] = (o * rs * subln_ref[...]).astype(BF16)

    @pl.when(n < n_lat_qblocks)
    def _():
        mixer_b(True)
        mixer_c(True)

    @pl.when(n >= n_lat_qblocks)
    def _():
        mixer_b(False)
        mixer_c(False)


def _mixers(u, vn, qb, kbt, vb, qc, kct, vc, sink, c_lam, wcat, abias, subln, lam_init, with_ctx):
    n_lat_qblocks = SEQ // ATT_ROWS
    n_qblocks = n_lat_qblocks + (1 if with_ctx else 0)
    lat_q_total = NLAT // ATT_ROWS
    out_rows = NTOK if with_ctx else NLAT

    def qrow(b, n):
        return (jnp.where(n < n_lat_qblocks, b * n_lat_qblocks + n, lat_q_total + b), 0)

    lat_rows = lambda b, n: (b, 0)
    ctx_rows = lambda b, n: (NLAT // CTX_LEN + b, 0)
    lat_cols = lambda b, n: (0, b)
    ctx_cols = lambda b, n: (0, NLAT // CTX_LEN + b)
    whole2 = lambda b, n: (0, 0)
    kernel = functools.partial(_mixer_kernel, lam_init=lam_init, n_lat_qblocks=n_lat_qblocks)
    return pl.pallas_call(
        kernel,
        grid=(BATCH, n_qblocks),
        in_specs=[
            pl.BlockSpec((ATT_ROWS, 256), qrow),
            pl.BlockSpec((ATT_ROWS, 256), qrow),
            pl.BlockSpec((ATT_ROWS, 512), qrow),
            pl.BlockSpec((ATT_ROWS, 256), qrow),
            pl.BlockSpec((256, SEQ), lat_cols),
            pl.BlockSpec((256, CTX_LEN), ctx_cols),
            pl.BlockSpec((SEQ, 512), lat_rows),
            pl.BlockSpec((CTX_LEN, 512), ctx_rows),
            pl.BlockSpec((256, SEQ), lat_cols),
            pl.BlockSpec((256, CTX_LEN), ctx_cols),
            pl.BlockSpec((SEQ, 512), lat_rows),
            pl.BlockSpec((CTX_LEN, 512), ctx_rows),
            pl.BlockSpec(memory_space=pltpu.SMEM),
            pl.BlockSpec((4, C_QK_DIM), whole2),
            pl.BlockSpec((2, CHUNK, 256), lambda b, n: (0, 0, 0)),
            pl.BlockSpec((CHUNK, 256), whole2),
            pl.BlockSpec((1, LANES), whole2),
        ],
        out_specs=pl.BlockSpec((ATT_ROWS, MIX_WIDTH), qrow),
        out_shape=jax.ShapeDtypeStruct((out_rows, MIX_WIDTH), BF16),
        compiler_params=_params(("arbitrary", "arbitrary"), 48),
        name="mixers",
    )(u, vn, qb, qc, kbt, kbt, vb, vb, kct, kct, vc, vc, sink, c_lam, wcat, abias, subln)


def _outproj_kernel(mix_ref, x_ref, g1_ref, sh_ref, sc_ref, ng_ref, w_ref, x1_ref, h2_ref):
    y = jnp.dot(mix_ref[...], w_ref[...], preferred_element_type=F32)
    x1 = x_ref[...] + g1_ref[0] * y
    x1_ref[...] = x1
    h2_ref[...] = _norm_modulate(x1, ng_ref[...], sh_ref[0], sc_ref[0]).astype(BF16)


def _outproj(mix, xs, mod, norm_g, w_out, n_rows):
    n_lat_blocks = NLAT // PROJ_ROWS
    blocks_per_seq = SEQ // PROJ_ROWS

    def cond_row(i):
        return jnp.where(i < n_lat_blocks, i // blocks_per_seq, CTX_COND_ROW)

    row = lambda i: (i, 0)
    whole = lambda i: (0, 0)
    modspec = lambda k: pl.BlockSpec((1, 1, D_MODEL), lambda i: (cond_row(i) * 6 + k, 0, 0))
    return pl.pallas_call(
        _outproj_kernel,
        grid=(n_rows // PROJ_ROWS,),
        in_specs=[
            pl.BlockSpec((PROJ_ROWS, MIX_WIDTH), row),
            pl.BlockSpec((PROJ_ROWS, D_MODEL), row),
            modspec(2), modspec(3), modspec(4),
            pl.BlockSpec((1, D_MODEL), whole),
            pl.BlockSpec((MIX_WIDTH, D_MODEL), whole),
        ],
        out_specs=(pl.BlockSpec((PROJ_ROWS, D_MODEL), row),
                   pl.BlockSpec((PROJ_ROWS, D_MODEL), row)),
        out_shape=(jax.ShapeDtypeStruct((n_rows, D_MODEL), F32),
                   jax.ShapeDtypeStruct((n_rows, D_MODEL), BF16)),
        compiler_params=_params(("arbitrary",), 40),
        name="outproj",
    )(mix, xs, mod, mod, mod, norm_g, w_out)


HALO = BF16_SUBLANES


FFN_HALF = FFN_ROWS // 2


def _ffn_kernel(*refs, context, blocks_per_seq):
    if context:
        (h_ref, x1_ref, g2_ref, mask_ref, wg_ref, wv_ref, cwg_ref, cwv_ref, cbg_ref, cbv_ref,
         wd_ref, _, o_ref, lhs_ref, *a_refs, y_ref) = refs
    else:
        (h_ref, hprev_ref, hnext_ref, x1_ref, g2_ref, wg_ref, wv_ref, cwg_ref, cwv_ref,
         cbg_ref, cbv_ref, wd_ref, o_ref, lhs_ref, *a_refs, y_ref) = refs
    i = pl.program_id(0)
    j = pl.program_id(1)

    @pl.when(j == 0)
    def _():
        lhs_ref[HALO:HALO + FFN_ROWS, :] = h_ref[...]
        zeros = jnp.zeros((HALO, D_MODEL), BF16)
        if context:
            lhs_ref[0:HALO, :] = zeros
            lhs_ref[HALO + FFN_ROWS:2 * HALO + FFN_ROWS, :] = zeros
        else:
            seq_pos = i % blocks_per_seq
            lhs_ref[0:HALO, :] = jnp.where(seq_pos == 0, zeros, hprev_ref[...])
            lhs_ref[HALO + FFN_ROWS:2 * HALO + FFN_ROWS, :] = jnp.where(
                seq_pos == blocks_per_seq - 1, zeros, hnext_ref[...])

    for half in range(2):
        lhs = lhs_ref[FFN_HALF * half:FFN_HALF * half + FFN_HALF + 2 * HALO, :]
        a_refs[2 * half][...] = jnp.dot(lhs, wg_ref[...], preferred_element_type=F32)
        a_refs[2 * half + 1][...] = jnp.dot(lhs, wv_ref[...], preferred_element_type=F32)

    def conv(a_ref, cw_ref, cb_ref, rows):
        prev = a_ref[HALO - 1:HALO - 1 + FFN_HALF, :]
        cur = a_ref[HALO:HALO + FFN_HALF, :]
        nxt = a_ref[HALO + 1:HALO + 1 + FFN_HALF, :]
        if context:
            prev = prev * mask_ref[0, rows, :]
            nxt = nxt * mask_ref[1, rows, :]
        return prev * cw_ref[0:1, :] + cur * cw_ref[1:2, :] + nxt * cw_ref[2:3, :] + cb_ref[...]

    for half in range(2):
        rows = slice(FFN_HALF * half, FFN_HALF * half + FFN_HALF)
        gate = conv(a_refs[2 * half], cwg_ref, cbg_ref, rows)
        val = conv(a_refs[2 * half + 1], cwv_ref, cbv_ref, rows)
        y_ref[j, rows, :] = ((gate / (1.0 + jnp.exp(-gate))) * val).astype(BF16)

    @pl.when(j == N_FF_TILES - 1)
    def _():
        acc = jnp.dot(y_ref[0], wd_ref[0], preferred_element_type=F32)
        for t in range(1, N_FF_TILES):
            acc += jnp.dot(y_ref[t], wd_ref[t], preferred_element_type=F32)
        o_ref[...] = x1_ref[...] + g2_ref[0] * acc


def _ffn(h2, x1, mod, masks, w_up, conv_w, conv_b, w_down, latent_out=None):
    context = latent_out is not None
    n_lat_blocks = NLAT // FFN_ROWS
    blocks_per_seq = SEQ // FFN_ROWS
    halo_per_block = FFN_ROWS // HALO
    first = n_lat_blocks if context else 0
    n_blocks = NCTX // FFN_ROWS if context else n_lat_blocks

    row = lambda i, j: (first + i, 0)
    whole3 = lambda i, j: (0, 0, 0)
    if context:
        g2 = pl.BlockSpec((1, 1, D_MODEL), lambda i, j: (CTX_COND_ROW * 6 + 5, 0, 0))
    else:
        g2 = pl.BlockSpec((1, 1, D_MODEL), lambda i, j: ((i // blocks_per_seq) * 6 + 5, 0, 0))
    weights = [
        pl.BlockSpec((D_MODEL, FF_TILE), lambda i, j: (0, j)),
        pl.BlockSpec((D_MODEL, FF_TILE), lambda i, j: (0, N_FF_TILES + j)),
        pl.BlockSpec((3, FF_TILE), lambda i, j: (0, j)),
        pl.BlockSpec((3, FF_TILE), lambda i, j: (0, N_FF_TILES + j)),
        pl.BlockSpec((1, FF_TILE), lambda i, j: (0, j)),
        pl.BlockSpec((1, FF_TILE), lambda i, j: (0, N_FF_TILES + j)),
        pl.BlockSpec((N_FF_TILES, FF_TILE, D_MODEL), whole3),
    ]
    weight_args = (w_up, w_up, conv_w, conv_w, conv_b, conv_b,
                   w_down.reshape(N_FF_TILES, FF_TILE, D_MODEL))
    main = pl.BlockSpec((FFN_ROWS, D_MODEL), row)
    if context:
        in_specs = [main, main, g2, pl.BlockSpec((2, FFN_ROWS, FF_TILE), whole3)] + weights + [
            pl.BlockSpec(memory_space=pl.ANY)]
        args = (h2, x1, mod, masks) + weight_args + (latent_out,)
        aliases = {len(args) - 1: 0}
    else:
        prev_halo = pl.BlockSpec((HALO, D_MODEL),
                                 lambda i, j: (jnp.maximum(i * halo_per_block - 1, 0), 0))
        next_halo = pl.BlockSpec((HALO, D_MODEL),
                                 lambda i, j: (jnp.minimum((i + 1) * halo_per_block,
                                                           NLAT // HALO - 1), 0))
        in_specs = [main, prev_halo, next_halo, main, g2] + weights
        args = (h2, h2, h2, x1, mod) + weight_args
        aliases = {}
    a_shape = pltpu.VMEM((FFN_HALF + 2 * HALO, FF_TILE), F32)
    return pl.pallas_call(
        functools.partial(_ffn_kernel, context=context, blocks_per_seq=blocks_per_seq),
        grid=(n_blocks, N_FF_TILES),
        in_specs=in_specs,
        out_specs=pl.BlockSpec((FFN_ROWS, D_MODEL), row),
        out_shape=jax.ShapeDtypeStruct(x1.shape, F32),
        scratch_shapes=[pltpu.VMEM((FFN_ROWS + 2 * HALO, D_MODEL), BF16),
                        a_shape, a_shape, a_shape, a_shape,
                        pltpu.VMEM((N_FF_TILES, FFN_ROWS, FF_TILE), BF16)],
        input_output_aliases=aliases,
        compiler_params=_params(("arbitrary", "arbitrary"), 56),
        name="ffn_ctx" if context else "ffn",
    )(*args)


def _rope_tables(head_dim):
    n_freq = head_dim // 4
    rows = SEQ // GRID_W
    row = jnp.repeat(jnp.arange(rows, dtype=F32), GRID_W)
    col = jnp.tile(jnp.arange(GRID_W, dtype=F32), rows)
    inv_freq = ROPE_BASE ** (-jnp.arange(n_freq, dtype=F32) / n_freq)
    ang = jnp.stack([row[:, None] * inv_freq, col[:, None] * inv_freq], axis=1)
    lane = np.arange(LANES)
    within = lane % head_dim
    axis = within // (2 * n_freq)
    second = (within % (2 * n_freq)) >= n_freq
    ang = ang[:, axis, lane % n_freq]
    cos = jnp.cos(ang)
    sin = jnp.where(jnp.asarray(second)[None, :], jnp.sin(ang), -jnp.sin(ang))
    cos = jnp.concatenate([cos, jnp.ones((PROJ_ROWS, LANES), F32)], axis=0)
    sin = jnp.concatenate([sin, jnp.zeros((PROJ_ROWS, LANES), F32)], axis=0)
    return cos, sin


def _group_matrix(width):
    idx = np.arange(256)
    g = (idx[:, None] // width == idx[None, :] // width).astype(np.float32) / width
    return jnp.asarray(g, dtype=BF16)


def _conv_masks():
    r = np.arange(FFN_ROWS)
    taps = (r % CTX_LEN != 0, r % CTX_LEN != CTX_LEN - 1)
    m = np.stack([np.broadcast_to(t[:, None], (FFN_ROWS, FF_TILE)) for t in taps])
    return jnp.asarray(m.astype(np.float32))


def _constants():
    cosb, sinb = _rope_tables(B_HEAD_DIM)
    cosc, sinc = _rope_tables(C_QK_DIM)
    return dict(cosb=cosb, sinb=sinb, cosc=cosc, sinc=sinc,
                g64=_group_matrix(64), g32=_group_matrix(32), masks=_conv_masks())


def kernel(x, c, ctx, c_ctx, w_ada, b_ada, norm1_g, norm2_g, w_in, a_ws, a_bs, b_qnorm, b_knorm,
           b_sink, c_qnorm, c_knorm, c_lam, c_subln, w_out, w_up, conv_w, conv_b, w_down):
    consts = _constants()
    xs = jnp.concatenate([x.reshape(NLAT, D_MODEL), ctx.reshape(NCTX, D_MODEL)], axis=0)

    cond = jnp.zeros((COND_ROWS, D_MODEL), F32).at[:BATCH].set(c).at[CTX_COND_ROW].set(c_ctx)
    mod_all = _modulation_all(cond, w_ada.astype(BF16), b_ada)
    mod_all = mod_all.reshape(DEPTH, COND_ROWS * 6, 1, D_MODEL)

    w_in_b = w_in.astype(BF16)
    w_out_b = w_out.astype(BF16)
    w_up_b = w_up.astype(BF16)
    w_down_b = w_down.astype(BF16)

    for i in range(DEPTH):
        last = i == DEPTH - 1
        lam_init = 0.8 - 0.6 * math.exp(-0.3 * i)
        mod = mod_all[i]
        zeros = jnp.zeros((4, LANES), F32)
        gains = jnp.concatenate([
            jnp.tile(b_qnorm[i] * (B_HEAD_DIM ** -0.5 * LOG2E), LANES // B_HEAD_DIM)[None],
            jnp.tile(b_knorm[i], LANES // B_HEAD_DIM)[None],
            jnp.tile(c_qnorm[i] * (C_QK_DIM ** -0.5 * LOG2E), LANES // C_QK_DIM)[None],
            jnp.tile(c_knorm[i], LANES // C_QK_DIM)[None],
            zeros], axis=0)
        u, vn, qb, kbt, vb, qc, kct, vc = _proj(
            xs, mod, norm1_g[i][None], w_in_b[i], consts, gains)

        wcat = jnp.concatenate([a_ws[i][0::2], a_ws[i][1::2]], axis=-1).astype(BF16)
        abias = jnp.repeat(a_bs[i].T, A_WIDTH // A_GROUPS, axis=1)
        subln = jnp.tile(c_subln[i] * (1.0 - lam_init), LANES // C_V_DIM)[None]
        mix = _mixers(u, vn, qb, kbt, vb, qc, kct, vc, b_sink[i], c_lam[i], wcat, abias, subln,
                      lam_init, with_ctx=not last)

        n_rows = NLAT if last else NTOK
        x1, h2 = _outproj(mix, xs, mod, norm2_g[i][None], w_out_b[i], n_rows)
        ffn_args = (h2, x1, mod, consts["masks"], w_up_b[i], conv_w[i].reshape(3, 2 * D_FF),
                    conv_b[i][None], w_down_b[i])
        xs = _ffn(*ffn_args)
        if not last:
            xs = _ffn(*ffn_args, latent_out=xs)

    return xs[:NLAT].reshape(BATCH, SEQ, D_MODEL)
```

```python
import functools
import math

import jax
import jax.numpy as jnp
import numpy as np
from jax import lax
from jax.experimental import pallas as pl
from jax.experimental.pallas import tpu as pltpu

F32 = jnp.float32
BF16 = jnp.bfloat16

D_MODEL = 1024
BATCH = 16
SEQ = 2048
DEPTH = 4
GRID_W = 64
CTX_LEN = 256
A_WIDTH = 256
A_GROUPS = 4
CHUNK = 128
B_HEAD_DIM = 64
B_HEADS = 8
B_KV_HEADS = 2
WINDOW = 128
C_V_DIM = 64
C_QK_DIM = 32
C_HEADS = 4
PROJ_WIDTH = 2048
MIX_WIDTH = 1024
D_FF = 2816
ROPE_BASE = 10000.0
EPS = 1e-6

LOG2E = 1.4426950408889634
NEG_BIG = -1e30

LANES = 128
BF16_SUBLANES = 16

NLAT = BATCH * SEQ
NCTX = BATCH * CTX_LEN
NTOK = NLAT + NCTX
COND_ROWS = 24
CTX_COND_ROW = BATCH

PROJ_ROWS = 512
ATT_ROWS = 256
FFN_ROWS = 512
FF_TILE = 256
N_FF_TILES = D_FF // FF_TILE
ADA_TILE = 1536

OFF_AU, OFF_AV, OFF_BQ, OFF_BK, OFF_BV, OFF_CQ, OFF_CK, OFF_CV = (
    0, 256, 512, 1024, 1152, 1280, 1536, 1792)


def _params(sem, vmem_mb):
    return pltpu.CompilerParams(dimension_semantics=sem,
                                vmem_limit_bytes=vmem_mb * 1024 * 1024)


def _mod_kernel(cond_ref, w_ref, b_ref, o_ref):
    c = cond_ref[...]
    s = c / (1.0 + jnp.exp(-c))
    o_ref[0] = jnp.dot(s.astype(BF16), w_ref[0], preferred_element_type=F32) + b_ref[0]


def _modulation_all(cond, w_ada, b_ada):
    n_tiles = (6 * D_MODEL) // ADA_TILE
    return pl.pallas_call(
        _mod_kernel,
        grid=(DEPTH, n_tiles),
        in_specs=[
            pl.BlockSpec((COND_ROWS, D_MODEL), lambda l, j: (0, 0)),
            pl.BlockSpec((1, D_MODEL, ADA_TILE), lambda l, j: (l, 0, j)),
            pl.BlockSpec((1, 1, ADA_TILE), lambda l, j: (l, 0, j)),
        ],
        out_specs=pl.BlockSpec((1, COND_ROWS, ADA_TILE), lambda l, j: (l, 0, j)),
        out_shape=jax.ShapeDtypeStruct((DEPTH, COND_ROWS, 6 * D_MODEL), F32),
        compiler_params=_params(("arbitrary", "arbitrary"), 32),
        name="modulation",
    )(cond, w_ada, b_ada.reshape(DEPTH, 1, 6 * D_MODEL))


def _norm_modulate(x, gain, shift, scale):
    ms = jnp.mean(x * x, axis=-1, keepdims=True)
    return (x * lax.rsqrt(ms + EPS) * gain) * (1.0 + scale) + shift


def _gelu_tanh(x):
    return 0.5 * x * (1.0 + jnp.tanh(math.sqrt(2.0 / math.pi) * (x + 0.044715 * (x * x * x))))


def _group_mean_sq(t, g):
    return jnp.dot((t * t).astype(BF16), g, preferred_element_type=F32)


def _swap_halves(t, half):
    lane = lax.broadcasted_iota(jnp.int32, t.shape, 1)
    first = (lane & (2 * half - 1)) < half
    return jnp.where(first, pltpu.roll(t, LANES - half, 1), pltpu.roll(t, half, 1))


def _lane_iota(shape):
    return lax.broadcasted_iota(jnp.int32, shape, 1)


def _proj_kernel(x_ref, sh_ref, sc_ref, ng_ref, w_ref, g64_ref, g32_ref,
                 cosb_ref, sinb_ref, cosc_ref, sinc_ref, gains_ref,
                 u_ref, vn_ref, qb_ref, kbt_ref, vb_ref, qc_ref, kct_ref, vc_ref):
    h = _norm_modulate(x_ref[...], ng_ref[...], sh_ref[0], sc_ref[0])
    z = jnp.dot(h.astype(BF16), w_ref[...], preferred_element_type=F32)

    u_ref[...] = _gelu_tanh(z[:, OFF_AU:OFF_AU + A_WIDTH])
    v = _gelu_tanh(z[:, OFF_AV:OFF_AV + A_WIDTH])
    vn_ref[...] = (v * lax.rsqrt(_group_mean_sq(v, g64_ref[...]) + EPS)).astype(BF16)

    def norm_rope(t, ms, gain, cos_ref, sin_ref, half):
        tg = t * gain
        return lax.rsqrt(ms + EPS) * (tg * cos_ref[...] + _swap_halves(tg, half) * sin_ref[...])

    for t in range(4):
        lo = OFF_BQ + 256 * (t // 2)
        if t % 2 == 0:
            q2 = z[:, lo:lo + 256]
            ms2 = _group_mean_sq(q2, g64_ref[...])
        s = slice(LANES * (t % 2), LANES * (t % 2) + LANES)
        qb_ref[:, LANES * t:LANES * t + LANES] = norm_rope(
            q2[:, s], ms2[:, s], gains_ref[0:1, :], cosb_ref, sinb_ref, 16).astype(BF16)

    kb = z[:, OFF_BK:OFF_BK + LANES]
    kb = norm_rope(kb, _group_mean_sq(kb, g64_ref[0:LANES, 0:LANES]),
                   gains_ref[1:2, :], cosb_ref, sinb_ref, 16)
    kbt = kb.T.astype(BF16)
    kbt_ref[0:LANES, :] = kbt
    kbt_ref[LANES:LANES + 64, :] = kbt[64:LANES, :]
    kbt_ref[LANES + 64:2 * LANES, :] = kbt[0:64, :]

    vb = z[:, OFF_BV:OFF_BV + LANES]
    ones = jnp.ones((PROJ_ROWS, LANES), BF16)
    vb_ref[:, 0:LANES] = vb.astype(BF16)
    vb_ref[:, LANES:2 * LANES] = ones
    vb_ref[:, 2 * LANES:3 * LANES] = pltpu.roll(vb, 64, 1).astype(BF16)
    vb_ref[:, 3 * LANES:4 * LANES] = ones

    cq = z[:, OFF_CQ:OFF_CQ + 256]
    msq = _group_mean_sq(cq, g32_ref[...])
    ck = z[:, OFF_CK:OFF_CK + 256]
    msk = _group_mean_sq(ck, g32_ref[...])
    for t in range(2):
        s = slice(LANES * t, LANES * t + LANES)
        qc_ref[:, s] = norm_rope(cq[:, s], msq[:, s], gains_ref[2:3, :],
                                 cosc_ref, sinc_ref, 8).astype(BF16)
        kc = norm_rope(ck[:, s], msk[:, s], gains_ref[3:4, :], cosc_ref, sinc_ref, 8)
        kct_ref[s, :] = kc.T.astype(BF16)

    cv = z[:, OFF_CV:OFF_CV + 256]
    vc_ref[:, 0:LANES] = cv[:, 0:LANES].astype(BF16)
    vc_ref[:, LANES:2 * LANES] = ones
    vc_ref[:, 2 * LANES:3 * LANES] = cv[:, LANES:2 * LANES].astype(BF16)
    vc_ref[:, 3 * LANES:4 * LANES] = ones


def _proj(xs, mod, norm_g, w_in, consts, gains):
    n_lat_blocks = NLAT // PROJ_ROWS
    blocks_per_seq = SEQ // PROJ_ROWS

    def cond_row(i):
        return jnp.where(i < n_lat_blocks, i // blocks_per_seq, CTX_COND_ROW)

    def pos_block(i):
        return jnp.where(i < n_lat_blocks, i % blocks_per_seq, blocks_per_seq)

    row = lambda i: (i, 0)
    col = lambda i: (0, i)
    whole = lambda i: (0, 0)
    pos = lambda i: (pos_block(i), 0)
    out_shape = (
        jax.ShapeDtypeStruct((NTOK, 256), F32),
        jax.ShapeDtypeStruct((NTOK, 256), BF16),
        jax.ShapeDtypeStruct((NTOK, 512), BF16),
        jax.ShapeDtypeStruct((256, NTOK), BF16),
        jax.ShapeDtypeStruct((NTOK, 512), BF16),
        jax.ShapeDtypeStruct((NTOK, 256), BF16),
        jax.ShapeDtypeStruct((256, NTOK), BF16),
        jax.ShapeDtypeStruct((NTOK, 512), BF16),
    )
    out_specs = (
        pl.BlockSpec((PROJ_ROWS, 256), row),
        pl.BlockSpec((PROJ_ROWS, 256), row),
        pl.BlockSpec((PROJ_ROWS, 512), row),
        pl.BlockSpec((256, PROJ_ROWS), col),
        pl.BlockSpec((PROJ_ROWS, 512), row),
        pl.BlockSpec((PROJ_ROWS, 256), row),
        pl.BlockSpec((256, PROJ_ROWS), col),
        pl.BlockSpec((PROJ_ROWS, 512), row),
    )
    return pl.pallas_call(
        _proj_kernel,
        grid=(NTOK // PROJ_ROWS,),
        in_specs=[
            pl.BlockSpec((PROJ_ROWS, D_MODEL), row),
            pl.BlockSpec((1, 1, D_MODEL), lambda i: (cond_row(i) * 6 + 0, 0, 0)),
            pl.BlockSpec((1, 1, D_MODEL), lambda i: (cond_row(i) * 6 + 1, 0, 0)),
            pl.BlockSpec((1, D_MODEL), whole),
            pl.BlockSpec((D_MODEL, PROJ_WIDTH), whole),
            pl.BlockSpec((256, 256), whole),
            pl.BlockSpec((256, 256), whole),
            pl.BlockSpec((PROJ_ROWS, LANES), pos),
            pl.BlockSpec((PROJ_ROWS, LANES), pos),
            pl.BlockSpec((PROJ_ROWS, LANES), pos),
            pl.BlockSpec((PROJ_ROWS, LANES), pos),
            pl.BlockSpec((8, LANES), whole),
        ],
        out_specs=out_specs,
        out_shape=out_shape,
        compiler_params=_params(("arbitrary",), 48),
        name="proj",
    )(xs, mod, mod, norm_g, w_in, consts["g64"], consts["g32"],
      consts["cosb"], consts["sinb"], consts["cosc"], consts["sinc"], gains)


def _exp2_bf16(s, m):
    return jnp.exp2(s - m).astype(BF16)


def _row_max(s):
    return jnp.max(s, axis=-1, keepdims=True)


def _mixer_kernel(u_ref, vn_ref, qb_ref, qc_ref,
                  kbl_ref, kbc_ref, vbl_ref, vbc_ref,
                  kcl_ref, kcc_ref, vcl_ref, vcc_ref,
                  sink_ref, lam_ref, wcat_ref, abias_ref, subln_ref,
                  mix_ref, *, lam_init, n_lat_qblocks):
    n = pl.program_id(1)
    lane = _lane_iota((ATT_ROWS, LANES))
    low_half = lane < 64
    dot = functools.partial(jnp.dot, preferred_element_type=F32)

    for c in range(ATT_ROWS // CHUNK):
        rows = slice(CHUNK * c, CHUNK * c + CHUNK)
        for t in range(2):
            cols = slice(LANES * t, LANES * t + LANES)
            vt = vn_ref[rows, cols].astype(F32)
            keep_lo = _lane_iota(vt.shape) < 64
            rhs = jnp.concatenate([jnp.where(keep_lo, vt, 0.0), jnp.where(keep_lo, 0.0, vt)],
                                  axis=0).astype(BF16)
            mixed = dot(wcat_ref[t], rhs) + abias_ref[:, cols]
            mix_ref[rows, cols] = (u_ref[rows, cols] * mixed).astype(BF16)

    lp = lam_ref[...]
    lam = (jnp.exp(jnp.sum(lp[0:1] * lp[1:2], axis=-1, keepdims=True))
           - jnp.exp(jnp.sum(lp[2:3] * lp[3:4], axis=-1, keepdims=True)) + lam_init)

    def head_mask(offset, width):
        return jnp.where((lane >= offset) & (lane < offset + width), 1.0, 0.0)

    def softmax_pv(scores, values, sink2):
        m = _row_max(scores[0])
        for s in scores[1:]:
            m = jnp.maximum(m, _row_max(s))
        if sink2 is not None:
            m = jnp.maximum(m, sink2)
        oa = dot(_exp2_bf16(scores[0], m), values[0])
        for s, v in zip(scores[1:], values[1:]):
            oa = oa + dot(_exp2_bf16(s, m), v)
        den = oa[:, LANES:2 * LANES]
        if sink2 is not None:
            den = den + jnp.exp2(sink2 - m)
        return oa[:, 0:LANES] / den

    def mixer_b(latent):
        if latent:
            start = pl.multiple_of(jnp.clip(n * ATT_ROWS - WINDOW, 0, SEQ - 2 * ATT_ROWS), LANES)
            qpos = n * ATT_ROWS + lax.broadcasted_iota(jnp.int32, (ATT_ROWS, 2 * ATT_ROWS), 0)
            kpos = start + lax.broadcasted_iota(jnp.int32, (ATT_ROWS, 2 * ATT_ROWS), 1)
            valid = jnp.abs(qpos - kpos) <= WINDOW
        for p in range(B_HEADS // 2):
            qt = qb_ref[:, LANES * p:LANES * p + LANES].astype(F32)
            outs = []
            for r in range(2):
                h = 2 * p + r
                j = h // (B_HEADS // B_KV_HEADS)
                var = 0 if j == r else 1
                krows = slice(LANES * var, LANES * var + LANES)
                vcols = slice(256 * var, 256 * var + 256)
                qm = (qt * head_mask(64 * r, 64)).astype(BF16)
                sink2 = sink_ref[h] * LOG2E
                s_ctx = dot(qm, kbc_ref[krows, :])
                if latent:
                    s_win = dot(qm, kbl_ref[krows, pl.ds(start, 2 * ATT_ROWS)])
                    s_win = jnp.where(valid, s_win, NEG_BIG)
                    outs.append(softmax_pv([s_win, s_ctx],
                                           [vbl_ref[pl.ds(start, 2 * ATT_ROWS), vcols],
                                            vbc_ref[:, vcols]], sink2))
                else:
                    outs.append(softmax_pv([s_ctx], [vbc_ref[:, vcols]], sink2))
            mix_ref[:, A_WIDTH + LANES * p:A_WIDTH + LANES * p + LANES] = (
                jnp.where(low_half, outs[0], outs[1]).astype(BF16))

    def mixer_c(latent):
        for t in range(C_HEADS // 2):
            qt = qc_ref[:, LANES * t:LANES * t + LANES].astype(F32)
            krows = slice(LANES * t, LANES * t + LANES)
            vcols = slice(256 * t, 256 * t + 256)
            outs = []
            for r in range(2):
                maps = []
                for mp in range(2):
                    qm = (qt * head_mask(64 * r + C_QK_DIM * mp, C_QK_DIM)).astype(BF16)
                    s_ctx = dot(qm, kcc_ref[krows, :])
                    if latent:
                        s_lat = dot(qm, kcl_ref[krows, :])
                        maps.append(softmax_pv([s_lat, s_ctx],
                                               [vcl_ref[:, vcols], vcc_ref[:, vcols]], None))
                    else:
                        maps.append(softmax_pv([s_ctx], [vcc_ref[:, vcols]], None))
                outs.append(maps[0] - lam * maps[1])
            o = jnp.where(low_half, outs[0], outs[1])
            o2 = o * o
            ms_lo = jnp.sum(jnp.where(low_half, o2, 0.0), axis=-1, keepdims=True) * (1.0 / C_V_DIM)
            ms_hi = jnp.sum(jnp.where(low_half, 0.0, o2), axis=-1, keepdims=True) * (1.0 / C_V_DIM)
            rs = jnp.where(low_half, lax.rsqrt(ms_lo + EPS), lax.rsqrt(ms_hi + EPS))
            off = A_WIDTH + 512 + LANES * t
            mix_ref[:, off:off + LANES] = (o * rs * subln_ref[...]).astype(BF16)

    @pl.when(n < n_lat_qblocks)
    def _():
        mixer_b(True)
        mixer_c(True)

    @pl.when(n >= n_lat_qblocks)
    def _():
        mixer_b(False)
        mixer_c(False)


def _mixers(u, vn, qb, kbt, vb, qc, kct, vc, sink, c_lam, wcat, abias, subln, lam_init, with_ctx):
    n_lat_qblocks = SEQ // ATT_ROWS
    n_qblocks = n_lat_qblocks + (1 if with_ctx else 0)
    lat_q_total = NLAT // ATT_ROWS
    out_rows = NTOK if with_ctx else NLAT

    def qrow(b, n):
        return (jnp.where(n < n_lat_qblocks, b * n_lat_qblocks + n, lat_q_total + b), 0)

    lat_rows = lambda b, n: (b, 0)
    ctx_rows = lambda b, n: (NLAT // CTX_LEN + b, 0)
    lat_cols = lambda b, n: (0, b)
    ctx_cols = lambda b, n: (0, NLAT // CTX_LEN + b)
    whole2 = lambda b, n: (0, 0)
    kernel = functools.partial(_mixer_kernel, lam_init=lam_init, n_lat_qblocks=n_lat_qblocks)
    return pl.pallas_call(
        kernel,
        grid=(BATCH, n_qblocks),
        in_specs=[
            pl.BlockSpec((ATT_ROWS, 256), qrow),
            pl.BlockSpec((ATT_ROWS, 256), qrow),
            pl.BlockSpec((ATT_ROWS, 512), qrow),
            pl.BlockSpec((ATT_ROWS, 256), qrow),
            pl.BlockSpec((256, SEQ), lat_cols),
            pl.BlockSpec((256, CTX_LEN), ctx_cols),
            pl.BlockSpec((SEQ, 512), lat_rows),
            pl.BlockSpec((CTX_LEN, 512), ctx_rows),
            pl.BlockSpec((256, SEQ), lat_cols),
            pl.BlockSpec((256, CTX_LEN), ctx_cols),
            pl.BlockSpec((SEQ, 512), lat_rows),
            pl.BlockSpec((CTX_LEN, 512), ctx_rows),
            pl.BlockSpec(memory_space=pltpu.SMEM),
            pl.BlockSpec((4, C_QK_DIM), whole2),
            pl.BlockSpec((2, CHUNK, 256), lambda b, n: (0, 0, 0)),
            pl.BlockSpec((CHUNK, 256), whole2),
            pl.BlockSpec((1, LANES), whole2),
        ],
        out_specs=pl.BlockSpec((ATT_ROWS, MIX_WIDTH), qrow),
        out_shape=jax.ShapeDtypeStruct((out_rows, MIX_WIDTH), BF16),
        compiler_params=_params(("arbitrary", "arbitrary"), 48),
        name="mixers",
    )(u, vn, qb, qc, kbt, kbt, vb, vb, kct, kct, vc, vc, sink, c_lam, wcat, abias, subln)


HALO = BF16_SUBLANES


def _ffn_kernel(*refs, context, blocks_per_seq):
    if context:
        (mix_ref, x_ref, g1_ref, sh_ref, sc_ref, g2_ref, ng_ref, mask_ref,
         wout_ref, wup_ref, cw_ref, cb_ref, wd_ref, _,
         o_ref, mext_ref, xext_ref, lhs_ref, *a_refs, y_ref) = refs
    else:
        (mix_ref, mixp_ref, mixn_ref, x_ref, xp_ref, xn_ref, g1_ref, sh_ref, sc_ref, g2_ref, ng_ref,
         wout_ref, wup_ref, cw_ref, cb_ref, wd_ref,
         o_ref, mext_ref, xext_ref, lhs_ref, *a_refs, y_ref) = refs
    top = slice(0, HALO)
    mid = slice(HALO, HALO + FFN_ROWS)
    bot = slice(HALO + FFN_ROWS, 2 * HALO + FFN_ROWS)

    mext_ref[mid, :] = mix_ref[...]
    xext_ref[mid, :] = x_ref[...]
    if context:
        mext_ref[top, :] = jnp.zeros((HALO, MIX_WIDTH), BF16)
        mext_ref[bot, :] = jnp.zeros((HALO, MIX_WIDTH), BF16)
        xext_ref[top, :] = jnp.zeros((HALO, D_MODEL), F32)
        xext_ref[bot, :] = jnp.zeros((HALO, D_MODEL), F32)
    else:
        mext_ref[top, :] = mixp_ref[...]
        mext_ref[bot, :] = mixn_ref[...]
        xext_ref[top, :] = xp_ref[...]
        xext_ref[bot, :] = xn_ref[...]

    x1 = xext_ref[...] + g1_ref[0] * jnp.dot(mext_ref[...], wout_ref[...],
                                             preferred_element_type=F32)
    xext_ref[...] = x1
    lhs_ref[...] = _norm_modulate(x1, ng_ref[...], sh_ref[0], sc_ref[0]).astype(BF16)
    zeros = jnp.zeros((HALO, D_MODEL), BF16)
    if context:
        lhs_ref[top, :] = zeros
        lhs_ref[bot, :] = zeros
    else:
        seq_pos = pl.program_id(0) % blocks_per_seq
        lhs_ref[top, :] = jnp.where(seq_pos == 0, zeros, lhs_ref[top, :])
        lhs_ref[bot, :] = jnp.where(seq_pos == blocks_per_seq - 1, zeros, lhs_ref[bot, :])

    def conv(a_ref, cols):
        prev = a_ref[HALO - 1:HALO - 1 + FFN_ROWS, :]
        cur = a_ref[HALO:HALO + FFN_ROWS, :]
        nxt = a_ref[HALO + 1:HALO + 1 + FFN_ROWS, :]
        if context:
            prev = prev * mask_ref[0]
            nxt = nxt * mask_ref[1]
        return (prev * cw_ref[0:1, cols] + cur * cw_ref[1:2, cols] + nxt * cw_ref[2:3, cols]
                + cb_ref[:, cols])

    for t in range(N_FF_TILES):
        gcols = slice(FF_TILE * t, FF_TILE * t + FF_TILE)
        vcols = slice(D_FF + FF_TILE * t, D_FF + FF_TILE * t + FF_TILE)
        ag_ref, av_ref = a_refs[2 * (t % 2)], a_refs[2 * (t % 2) + 1]
        ag_ref[...] = jnp.dot(lhs_ref[...], wup_ref[:, gcols], preferred_element_type=F32)
        av_ref[...] = jnp.dot(lhs_ref[...], wup_ref[:, vcols], preferred_element_type=F32)
        gate = conv(ag_ref, gcols)
        val = conv(av_ref, vcols)
        y_ref[:, gcols] = ((gate / (1.0 + jnp.exp(-gate))) * val).astype(BF16)

    acc = jnp.dot(y_ref[...], wd_ref[...], preferred_element_type=F32)
    o_ref[...] = xext_ref[mid, :] + g2_ref[0] * acc


def _ffn(mix, xs, mod, norm_g, masks, w_out, w_up, conv_w, conv_b, w_down, latent_out=None):
    context = latent_out is not None
    n_lat_blocks = NLAT // FFN_ROWS
    blocks_per_seq = SEQ // FFN_ROWS
    halo_per_block = FFN_ROWS // HALO
    first = n_lat_blocks if context else 0
    n_blocks = NCTX // FFN_ROWS if context else n_lat_blocks
    ext_rows = FFN_ROWS + 2 * HALO

    row = lambda i: (first + i, 0)
    whole = lambda i: (0, 0)
    resident = dict(pipeline_mode=pl.Buffered(1))

    def cond_row(i):
        return CTX_COND_ROW if context else i // blocks_per_seq

    modspec = lambda k: pl.BlockSpec((1, 1, D_MODEL), lambda i: (cond_row(i) * 6 + k, 0, 0))
    mods = [modspec(2), modspec(3), modspec(4), modspec(5), pl.BlockSpec((1, D_MODEL), whole)]
    mod_args = (mod, mod, mod, mod, norm_g)
    weights = [
        pl.BlockSpec((MIX_WIDTH, D_MODEL), whole, **resident),
        pl.BlockSpec((D_MODEL, 2 * D_FF), whole, **resident),
        pl.BlockSpec((3, 2 * D_FF), whole, **resident),
        pl.BlockSpec((1, 2 * D_FF), whole, **resident),
        pl.BlockSpec((D_FF, D_MODEL), whole, **resident),
    ]
    weight_args = (w_out, w_up, conv_w, conv_b, w_down)
    main = pl.BlockSpec((FFN_ROWS, D_MODEL), row)
    if context:
        in_specs = [main, main] + mods + [
            pl.BlockSpec((2, FFN_ROWS, FF_TILE), lambda i: (0, 0, 0), **resident)
        ] + weights + [pl.BlockSpec(memory_space=pl.ANY)]
        args = (mix, xs) + mod_args + (masks,) + weight_args + (latent_out,)
        aliases = {len(args) - 1: 0}
        out_rows = latent_out.shape[0]
    else:
        prev_halo = pl.BlockSpec((HALO, D_MODEL),
                                 lambda i: (jnp.maximum(i * halo_per_block - 1, 0), 0))
        next_halo = pl.BlockSpec((HALO, D_MODEL),
                                 lambda i: (jnp.minimum((i + 1) * halo_per_block,
                                                        NLAT // HALO - 1), 0))
        in_specs = [main, prev_halo, next_halo, main, prev_halo, next_halo] + mods + weights
        args = (mix, mix, mix, xs, xs, xs) + mod_args + weight_args
        aliases = {}
        out_rows = mix.shape[0]
    a_shape = pltpu.VMEM((ext_rows, FF_TILE), F32)
    return pl.pallas_call(
        functools.partial(_ffn_kernel, context=context, blocks_per_seq=blocks_per_seq),
        grid=(n_blocks,),
        in_specs=in_specs,
        out_specs=pl.BlockSpec((FFN_ROWS, D_MODEL), row),
        out_shape=jax.ShapeDtypeStruct((out_rows, D_MODEL), F32),
        scratch_shapes=[pltpu.VMEM((ext_rows, MIX_WIDTH), BF16),
                        pltpu.VMEM((ext_rows, D_MODEL), F32),
                        pltpu.VMEM((ext_rows, D_MODEL), BF16),
                        a_shape, a_shape, a_shape, a_shape,
                        pltpu.VMEM((FFN_ROWS, D_FF), BF16)],
        input_output_aliases=aliases,
        compiler_params=_params(("arbitrary",), 56),
        name="ffn_ctx" if context else "ffn",
    )(*args)


def _rope_tables(head_dim):
    n_freq = head_dim // 4
    rows = SEQ // GRID_W
    row = jnp.repeat(jnp.arange(rows, dtype=F32), GRID_W)
    col = jnp.tile(jnp.arange(GRID_W, dtype=F32), rows)
    inv_freq = ROPE_BASE ** (-jnp.arange(n_freq, dtype=F32) / n_freq)
    ang = jnp.stack([row[:, None] * inv_freq, col[:, None] * inv_freq], axis=1)
    lane = np.arange(LANES)
    within = lane % head_dim
    axis = within // (2 * n_freq)
    second = (within % (2 * n_freq)) >= n_freq
    ang = ang[:, axis, lane % n_freq]
    cos = jnp.cos(ang)
    sin = jnp.where(jnp.asarray(second)[None, :], jnp.sin(ang), -jnp.sin(ang))
    cos = jnp.concatenate([cos, jnp.ones((PROJ_ROWS, LANES), F32)], axis=0)
    sin = jnp.concatenate([sin, jnp.zeros((PROJ_ROWS, LANES), F32)], axis=0)
    return cos, sin


def _group_matrix(width):
    idx = np.arange(256)
    g = (idx[:, None] // width == idx[None, :] // width).astype(np.float32) / width
    return jnp.asarray(g, dtype=BF16)


def _conv_masks():
    r = np.arange(FFN_ROWS)
    taps = (r % CTX_LEN != 0, r % CTX_LEN != CTX_LEN - 1)
    m = np.stack([np.broadcast_to(t[:, None], (FFN_ROWS, FF_TILE)) for t in taps])
    return jnp.asarray(m.astype(np.float32))


def _constants():
    cosb, sinb = _rope_tables(B_HEAD_DIM)
    cosc, sinc = _rope_tables(C_QK_DIM)
    return dict(cosb=cosb, sinb=sinb, cosc=cosc, sinc=sinc,
                g64=_group_matrix(64), g32=_group_matrix(32), masks=_conv_masks())


def kernel(x, c, ctx, c_ctx, w_ada, b_ada, norm1_g, norm2_g, w_in, a_ws, a_bs, b_qnorm, b_knorm,
           b_sink, c_qnorm, c_knorm, c_lam, c_subln, w_out, w_up, conv_w, conv_b, w_down):
    consts = _constants()
    xs = jnp.concatenate([x.reshape(NLAT, D_MODEL), ctx.reshape(NCTX, D_MODEL)], axis=0)

    cond = jnp.zeros((COND_ROWS, D_MODEL), F32).at[:BATCH].set(c).at[CTX_COND_ROW].set(c_ctx)
    mod_all = _modulation_all(cond, w_ada.astype(BF16), b_ada)
    mod_all = mod_all.reshape(DEPTH, COND_ROWS * 6, 1, D_MODEL)

    w_in_b = w_in.astype(BF16)
    w_out_b = w_out.astype(BF16)
    w_up_b = w_up.astype(BF16)
    w_down_b = w_down.astype(BF16)

    for i in range(DEPTH):
        last = i == DEPTH - 1
        lam_init = 0.8 - 0.6 * math.exp(-0.3 * i)
        mod = mod_all[i]
        zeros = jnp.zeros((4, LANES), F32)
        gains = jnp.concatenate([
            jnp.tile(b_qnorm[i] * (B_HEAD_DIM ** -0.5 * LOG2E), LANES // B_HEAD_DIM)[None],
            jnp.tile(b_knorm[i], LANES // B_HEAD_DIM)[None],
            jnp.tile(c_qnorm[i] * (C_QK_DIM ** -0.5 * LOG2E), LANES // C_QK_DIM)[None],
            jnp.tile(c_knorm[i], LANES // C_QK_DIM)[None],
            zeros], axis=0)
        u, vn, qb, kbt, vb, qc, kct, vc = _proj(
            xs, mod, norm1_g[i][None], w_in_b[i], consts, gains)

        wcat = jnp.concatenate([a_ws[i][0::2], a_ws[i][1::2]], axis=-1).astype(BF16)
        abias = jnp.repeat(a_bs[i].T, A_WIDTH // A_GROUPS, axis=1)
        subln = jnp.tile(c_subln[i] * (1.0 - lam_init), LANES // C_V_DIM)[None]
        mix = _mixers(u, vn, qb, kbt, vb, qc, kct, vc, b_sink[i], c_lam[i], wcat, abias, subln,
                      lam_init, with_ctx=not last)

        ffn_args = (mix, xs, mod, norm2_g[i][None], consts["masks"], w_out_b[i], w_up_b[i],
                    conv_w[i].reshape(3, 2 * D_FF), conv_b[i][None], w_down_b[i])
        xs = _ffn(*ffn_args)
        if not last:
            xs = _ffn(*ffn_args, latent_out=xs)

    return xs[:NLAT].reshape(BATCH, SEQ, D_MODEL)
```

```python
import functools
import math

import jax
import jax.numpy as jnp
import numpy as np
from jax import lax
from jax.experimental import pallas as pl
from jax.experimental.pallas import tpu as pltpu

F32 = jnp.float32
BF16 = jnp.bfloat16

D_MODEL = 1024
BATCH = 16
SEQ = 2048
DEPTH = 4
GRID_W = 64
CTX_LEN = 256
A_WIDTH = 256
A_GROUPS = 4
CHUNK = 128
B_HEAD_DIM = 64
B_HEADS = 8
B_KV_HEADS = 2
WINDOW = 128
C_V_DIM = 64
C_QK_DIM = 32
C_HEADS = 4
PROJ_WIDTH = 2048
MIX_WIDTH = 1024
D_FF = 2816
ROPE_BASE = 10000.0
EPS = 1e-6

LOG2E = 1.4426950408889634
NEG_BIG = -1e30

LANES = 128
BF16_SUBLANES = 16

NLAT = BATCH * SEQ
NCTX = BATCH * CTX_LEN
NTOK = NLAT + NCTX
COND_ROWS = 24
CTX_COND_ROW = BATCH

PROJ_ROWS = 512
ATT_ROWS = 256
FFN_ROWS = 512
FF_TILE = 256
N_FF_TILES = D_FF // FF_TILE
ADA_TILE = 1536

OFF_AU, OFF_AV, OFF_BQ, OFF_BK, OFF_BV, OFF_CQ, OFF_CK, OFF_CV = (
    0, 256, 512, 1024, 1152, 1280, 1536, 1792)


def _params(sem, vmem_mb):
    return pltpu.CompilerParams(dimension_semantics=sem,
                                vmem_limit_bytes=vmem_mb * 1024 * 1024)


def _mod_kernel(cond_ref, w_ref, b_ref, o_ref):
    c = cond_ref[...]
    s = c / (1.0 + jnp.exp(-c))
    o_ref[0] = jnp.dot(s.astype(BF16), w_ref[0], preferred_element_type=F32) + b_ref[0]


def _modulation_all(cond, w_ada, b_ada):
    n_tiles = (6 * D_MODEL) // ADA_TILE
    return pl.pallas_call(
        _mod_kernel,
        grid=(DEPTH, n_tiles),
        in_specs=[
            pl.BlockSpec((COND_ROWS, D_MODEL), lambda l, j: (0, 0)),
            pl.BlockSpec((1, D_MODEL, ADA_TILE), lambda l, j: (l, 0, j)),
            pl.BlockSpec((1, 1, ADA_TILE), lambda l, j: (l, 0, j)),
        ],
        out_specs=pl.BlockSpec((1, COND_ROWS, ADA_TILE), lambda l, j: (l, 0, j)),
        out_shape=jax.ShapeDtypeStruct((DEPTH, COND_ROWS, 6 * D_MODEL), F32),
        compiler_params=_params(("arbitrary", "arbitrary"), 32),
        name="modulation",
    )(cond, w_ada, b_ada.reshape(DEPTH, 1, 6 * D_MODEL))


def _norm_modulate(x, gain, shift, scale):
    ms = jnp.mean(x * x, axis=-1, keepdims=True)
    return (x * lax.rsqrt(ms + EPS) * gain) * (1.0 + scale) + shift


def _gelu_tanh(x):
    return 0.5 * x * (1.0 + jnp.tanh(math.sqrt(2.0 / math.pi) * (x + 0.044715 * (x * x * x))))


def _group_mean_sq(t, g):
    return jnp.dot((t * t).astype(BF16), g, preferred_element_type=F32)


def _swap_halves(t, half):
    lane = lax.broadcasted_iota(jnp.int32, t.shape, 1)
    first = (lane & (2 * half - 1)) < half
    return jnp.where(first, pltpu.roll(t, LANES - half, 1), pltpu.roll(t, half, 1))


def _lane_iota(shape):
    return lax.broadcasted_iota(jnp.int32, shape, 1)


def _proj_kernel(x_ref, sh_ref, sc_ref, ng_ref, w_ref, g64_ref, g32_ref,
                 cosb_ref, sinb_ref, cosc_ref, sinc_ref, gains_ref,
                 u_ref, vn_ref, qb_ref, kbt_ref, vb_ref, qc_ref, kct_ref, vc_ref):
    h = _norm_modulate(x_ref[...], ng_ref[...], sh_ref[0], sc_ref[0])
    z = jnp.dot(h.astype(BF16), w_ref[...], preferred_element_type=F32)

    u_ref[...] = _gelu_tanh(z[:, OFF_AU:OFF_AU + A_WIDTH])
    v = _gelu_tanh(z[:, OFF_AV:OFF_AV + A_WIDTH])
    vn_ref[...] = (v * lax.rsqrt(_group_mean_sq(v, g64_ref[...]) + EPS)).astype(BF16)

    def norm_rope(t, ms, gain, cos_ref, sin_ref, half):
        tg = t * gain
        return lax.rsqrt(ms + EPS) * (tg * cos_ref[...] + _swap_halves(tg, half) * sin_ref[...])

    for t in range(4):
        lo = OFF_BQ + 256 * (t // 2)
        if t % 2 == 0:
            q2 = z[:, lo:lo + 256]
            ms2 = _group_mean_sq(q2, g64_ref[...])
        s = slice(LANES * (t % 2), LANES * (t % 2) + LANES)
        qb_ref[:, LANES * t:LANES * t + LANES] = norm_rope(
            q2[:, s], ms2[:, s], gains_ref[0:1, :], cosb_ref, sinb_ref, 16).astype(BF16)

    kb = z[:, OFF_BK:OFF_BK + LANES]
    kb = norm_rope(kb, _group_mean_sq(kb, g64_ref[0:LANES, 0:LANES]),
                   gains_ref[1:2, :], cosb_ref, sinb_ref, 16)
    kbt = kb.T.astype(BF16)
    kbt_ref[0:LANES, :] = kbt
    kbt_ref[LANES:LANES + 64, :] = kbt[64:LANES, :]
    kbt_ref[LANES + 64:2 * LANES, :] = kbt[0:64, :]

    vb = z[:, OFF_BV:OFF_BV + LANES]
    ones = jnp.ones((PROJ_ROWS, LANES), BF16)
    vb_ref[:, 0:LANES] = vb.astype(BF16)
    vb_ref[:, LANES:2 * LANES] = ones
    vb_ref[:, 2 * LANES:3 * LANES] = pltpu.roll(vb, 64, 1).astype(BF16)
    vb_ref[:, 3 * LANES:4 * LANES] = ones

    cq = z[:, OFF_CQ:OFF_CQ + 256]
    msq = _group_mean_sq(cq, g32_ref[...])
    ck = z[:, OFF_CK:OFF_CK + 256]
    msk = _group_mean_sq(ck, g32_ref[...])
    for t in range(2):
        s = slice(LANES * t, LANES * t + LANES)
        qc_ref[:, s] = norm_rope(cq[:, s], msq[:, s], gains_ref[2:3, :],
                                 cosc_ref, sinc_ref, 8).astype(BF16)
        kc = norm_rope(ck[:, s], msk[:, s], gains_ref[3:4, :], cosc_ref, sinc_ref, 8)
        kct_ref[s, :] = kc.T.astype(BF16)

    cv = z[:, OFF_CV:OFF_CV + 256]
    vc_ref[:, 0:LANES] = cv[:, 0:LANES].astype(BF16)
    vc_ref[:, LANES:2 * LANES] = ones
    vc_ref[:, 2 * LANES:3 * LANES] = cv[:, LANES:2 * LANES].astype(BF16)
    vc_ref[:, 3 * LANES:4 * LANES] = ones


def _proj(xs, mod, norm_g, w_in, consts, gains):
    n_lat_blocks = NLAT // PROJ_ROWS
    blocks_per_seq = SEQ // PROJ_ROWS

    def cond_row(i):
        return jnp.where(i < n_lat_blocks, i // blocks_per_seq, CTX_COND_ROW)

    def pos_block(i):
        return jnp.where(i < n_lat_blocks, i % blocks_per_seq, blocks_per_seq)

    row = lambda i: (i, 0)
    col = lambda i: (0, i)
    whole = lambda i: (0, 0)
    pos = lambda i: (pos_block(i), 0)
    out_shape = (
        jax.ShapeDtypeStruct((NTOK, 256), F32),
        jax.ShapeDtypeStruct((NTOK, 256), BF16),
        jax.ShapeDtypeStruct((NTOK, 512), BF16),
        jax.ShapeDtypeStruct((256, NTOK), BF16),
        jax.ShapeDtypeStruct((NTOK, 512), BF16),
        jax.ShapeDtypeStruct((NTOK, 256), BF16),
        jax.ShapeDtypeStruct((256, NTOK), BF16),
        jax.ShapeDtypeStruct((NTOK, 512), BF16),
    )
    out_specs = (
        pl.BlockSpec((PROJ_ROWS, 256), row),
        pl.BlockSpec((PROJ_ROWS, 256), row),
        pl.BlockSpec((PROJ_ROWS, 512), row),
        pl.BlockSpec((256, PROJ_ROWS), col),
        pl.BlockSpec((PROJ_ROWS, 512), row),
        pl.BlockSpec((PROJ_ROWS, 256), row),
        pl.BlockSpec((256, PROJ_ROWS), col),
        pl.BlockSpec((PROJ_ROWS, 512), row),
    )
    return pl.pallas_call(
        _proj_kernel,
        grid=(NTOK // PROJ_ROWS,),
        in_specs=[
            pl.BlockSpec((PROJ_ROWS, D_MODEL), row),
            pl.BlockSpec((1, 1, D_MODEL), lambda i: (cond_row(i) * 6 + 0, 0, 0)),
            pl.BlockSpec((1, 1, D_MODEL), lambda i: (cond_row(i) * 6 + 1, 0, 0)),
            pl.BlockSpec((1, D_MODEL), whole),
            pl.BlockSpec((D_MODEL, PROJ_WIDTH), whole),
            pl.BlockSpec((256, 256), whole),
            pl.BlockSpec((256, 256), whole),
            pl.BlockSpec((PROJ_ROWS, LANES), pos),
            pl.BlockSpec((PROJ_ROWS, LANES), pos),
            pl.BlockSpec((PROJ_ROWS, LANES), pos),
            pl.BlockSpec((PROJ_ROWS, LANES), pos),
            pl.BlockSpec((8, LANES), whole),
        ],
        out_specs=out_specs,
        out_shape=out_shape,
        compiler_params=_params(("arbitrary",), 48),
        name="proj",
    )(xs, mod, mod, norm_g, w_in, consts["g64"], consts["g32"],
      consts["cosb"], consts["sinb"], consts["cosc"], consts["sinc"], gains)


def _exp2_bf16(s, m):
    return jnp.exp2(s - m).astype(BF16)


def _row_max(s):
    return jnp.max(s, axis=-1, keepdims=True)


def _mixer_kernel(u_ref, vn_ref, qb_ref, qc_ref,
                  kbl_ref, kbc_ref, vbl_ref, vbc_ref,
                  kcl_ref, kcc_ref, vcl_ref, vcc_ref,
                  sink_ref, lam_ref, wcat_ref, abias_ref, subln_ref,
                  mix_ref, *, lam_init, n_lat_qblocks):
    n = pl.program_id(1)
    lane = _lane_iota((ATT_ROWS, LANES))
    low_half = lane < 64
    dot = functools.partial(jnp.dot, preferred_element_type=F32)

    def mixer_a():
        for c in range(ATT_ROWS // CHUNK):
            rows = slice(CHUNK * c, CHUNK * c + CHUNK)
            for t in range(2):
                cols = slice(LANES * t, LANES * t + LANES)
                vt = vn_ref[rows, cols].astype(F32)
                keep_lo = _lane_iota(vt.shape) < 64
                rhs = jnp.concatenate([jnp.where(keep_lo, vt, 0.0), jnp.where(keep_lo, 0.0, vt)],
                                      axis=0).astype(BF16)
                mixed = dot(wcat_ref[t], rhs) + abias_ref[:, cols]
                mix_ref[rows, cols] = (u_ref[rows, cols] * mixed).astype(BF16)

    lp = lam_ref[...]
    lam = (jnp.exp(jnp.sum(lp[0:1] * lp[1:2], axis=-1, keepdims=True))
           - jnp.exp(jnp.sum(lp[2:3] * lp[3:4], axis=-1, keepdims=True)) + lam_init)

    def head_mask(offset, width):
        return jnp.where((lane >= offset) & (lane < offset + width), 1.0, 0.0)

    def softmax_pv(scores, values, sink2):
        m = _row_max(scores[0])
        for s in scores[1:]:
            m = jnp.maximum(m, _row_max(s))
        if sink2 is not None:
            m = jnp.maximum(m, sink2)
        oa = dot(_exp2_bf16(scores[0], m), values[0])
        for s, v in zip(scores[1:], values[1:]):
            oa = oa + dot(_exp2_bf16(s, m), v)
        den = oa[:, LANES:2 * LANES]
        if sink2 is not None:
            den = den + jnp.exp2(sink2 - m)
        return oa[:, 0:LANES] / den

    def mixer_b(latent):
        if latent:
            start = pl.multiple_of(jnp.clip(n * ATT_ROWS - WINDOW, 0, SEQ - 2 * ATT_ROWS), LANES)
            qpos = n * ATT_ROWS + lax.broadcasted_iota(jnp.int32, (ATT_ROWS, 2 * ATT_ROWS), 0)
            kpos = start + lax.broadcasted_iota(jnp.int32, (ATT_ROWS, 2 * ATT_ROWS), 1)
            valid = jnp.abs(qpos - kpos) <= WINDOW
        for p in range(B_HEADS // 2):
            qt = qb_ref[:, LANES * p:LANES * p + LANES].astype(F32)
            outs = []
            for r in range(2):
                h = 2 * p + r
                j = h // (B_HEADS // B_KV_HEADS)
                var = 0 if j == r else 1
                krows = slice(LANES * var, LANES * var + LANES)
                vcols = slice(256 * var, 256 * var + 256)
                qm = (qt * head_mask(64 * r, 64)).astype(BF16)
                sink2 = sink_ref[h] * LOG2E
                s_ctx = dot(qm, kbc_ref[krows, :])
                if latent:
                    s_win = dot(qm, kbl_ref[krows, pl.ds(start, 2 * ATT_ROWS)])
                    s_win = jnp.where(valid, s_win, NEG_BIG)
                    outs.append(softmax_pv([s_win, s_ctx],
                                           [vbl_ref[pl.ds(start, 2 * ATT_ROWS), vcols],
                                            vbc_ref[:, vcols]], sink2))
                else:
                    outs.append(softmax_pv([s_ctx], [vbc_ref[:, vcols]], sink2))
            mix_ref[:, A_WIDTH + LANES * p:A_WIDTH + LANES * p + LANES] = (
                jnp.where(low_half, outs[0], outs[1]).astype(BF16))

    def mixer_c(latent):
        for t in range(C_HEADS // 2):
            qt = qc_ref[:, LANES * t:LANES * t + LANES].astype(F32)
            krows = slice(LANES * t, LANES * t + LANES)
            vcols = slice(256 * t, 256 * t + 256)
            outs = []
            for r in range(2):
                maps = []
                for mp in range(2):
                    qm = (qt * head_mask(64 * r + C_QK_DIM * mp, C_QK_DIM)).astype(BF16)
                    s_ctx = dot(qm, kcc_ref[krows, :])
                    if latent:
                        s_lat = dot(qm, kcl_ref[krows, :])
                        maps.append(softmax_pv([s_lat, s_ctx],
                                               [vcl_ref[:, vcols], vcc_ref[:, vcols]], None))
                    else:
                        maps.append(softmax_pv([s_ctx], [vcc_ref[:, vcols]], None))
                outs.append(maps[0] - lam * maps[1])
            o = jnp.where(low_half, outs[0], outs[1])
            o2 = o * o
            ms_lo = jnp.sum(jnp.where(low_half, o2, 0.0), axis=-1, keepdims=True) * (1.0 / C_V_DIM)
            ms_hi = jnp.sum(jnp.where(low_half, 0.0, o2), axis=-1, keepdims=True) * (1.0 / C_V_DIM)
            rs = jnp.where(low_half, lax.rsqrt(ms_lo + EPS), lax.rsqrt(ms_hi + EPS))
            off = A_WIDTH + 512 + LANES * t
            mix_ref[:, off:off + LANES] = (o * rs * subln_ref[...]).astype(BF16)

    @pl.when(n < n_lat_qblocks)
    def _():
        mixer_c(True)
        mixer_b(True)
        mixer_a()

    @pl.when(n >= n_lat_qblocks)
    def _():
        mixer_c(False)
        mixer_b(False)
        mixer_a()


def _mixers(u, vn, qb, kbt, vb, qc, kct, vc, sink, c_lam, wcat, abias, subln, lam_init, with_ctx):
    n_lat_qblocks = SEQ // ATT_ROWS
    n_qblocks = n_lat_qblocks + (1 if with_ctx else 0)
    lat_q_total = NLAT // ATT_ROWS
    out_rows = NTOK if with_ctx else NLAT

    def qrow(b, n):
        return (jnp.where(n < n_lat_qblocks, b * n_lat_qblocks + n, lat_q_total + b), 0)

    lat_rows = lambda b, n: (b, 0)
    ctx_rows = lambda b, n: (NLAT // CTX_LEN + b, 0)
    lat_cols = lambda b, n: (0, b)
    ctx_cols = lambda b, n: (0, NLAT // CTX_LEN + b)
    whole2 = lambda b, n: (0, 0)
    kernel = functools.partial(_mixer_kernel, lam_init=lam_init, n_lat_qblocks=n_lat_qblocks)
    return pl.pallas_call(
        kernel,
        grid=(BATCH, n_qblocks),
        in_specs=[
            pl.BlockSpec((ATT_ROWS, 256), qrow),
            pl.BlockSpec((ATT_ROWS, 256), qrow),
            pl.BlockSpec((ATT_ROWS, 512), qrow),
            pl.BlockSpec((ATT_ROWS, 256), qrow),
            pl.BlockSpec((256, SEQ), lat_cols),
            pl.BlockSpec((256, CTX_LEN), ctx_cols),
            pl.BlockSpec((SEQ, 512), lat_rows),
            pl.BlockSpec((CTX_LEN, 512), ctx_rows),
            pl.BlockSpec((256, SEQ), lat_cols),
            pl.BlockSpec((256, CTX_LEN), ctx_cols),
            pl.BlockSpec((SEQ, 512), lat_rows),
            pl.BlockSpec((CTX_LEN, 512), ctx_rows),
            pl.BlockSpec(memory_space=pltpu.SMEM),
            pl.BlockSpec((4, C_QK_DIM), whole2),
            pl.BlockSpec((2, CHUNK, 256), lambda b, n: (0, 0, 0)),
            pl.BlockSpec((CHUNK, 256), whole2),
            pl.BlockSpec((1, LANES), whole2),
        ],
        out_specs=pl.BlockSpec((ATT_ROWS, MIX_WIDTH), qrow),
        out_shape=jax.ShapeDtypeStruct((out_rows, MIX_WIDTH), BF16),
        compiler_params=_params(("arbitrary", "arbitrary"), 48),
        name="mixers",
    )(u, vn, qb, qc, kbt, kbt, vb, vb, kct, kct, vc, vc, sink, c_lam, wcat, abias, subln)


HALO = BF16_SUBLANES


def _ffn_kernel(*refs, context, blocks_per_seq):
    if context:
        (mix_ref, x_ref, g1_ref, sh_ref, sc_ref, g2_ref, ng_ref, mask_ref,
         wout_ref, wup_ref, cw_ref, cb_ref, wd_ref, _,
         o_ref, mext_ref, xext_ref, lhs_ref, *a_refs, y_ref) = refs
    else:
        (mix_ref, mixp_ref, mixn_ref, x_ref, xp_ref, xn_ref, g1_ref, sh_ref, sc_ref, g2_ref, ng_ref,
         wout_ref, wup_ref, cw_ref, cb_ref, wd_ref,
         o_ref, mext_ref, xext_ref, lhs_ref, *a_refs, y_ref) = refs
    top = slice(0, HALO)
    mid = slice(HALO, HALO + FFN_ROWS)
    bot = slice(HALO + FFN_ROWS, 2 * HALO + FFN_ROWS)

    mext_ref[mid, :] = mix_ref[...]
    xext_ref[mid, :] = x_ref[...]
    if context:
        mext_ref[top, :] = jnp.zeros((HALO, MIX_WIDTH), BF16)
        mext_ref[bot, :] = jnp.zeros((HALO, MIX_WIDTH), BF16)
        xext_ref[top, :] = jnp.zeros((HALO, D_MODEL), F32)
        xext_ref[bot, :] = jnp.zeros((HALO, D_MODEL), F32)
    else:
        mext_ref[top, :] = mixp_ref[...]
        mext_ref[bot, :] = mixn_ref[...]
        xext_ref[top, :] = xp_ref[...]
        xext_ref[bot, :] = xn_ref[...]

    x1 = xext_ref[...] + g1_ref[0] * jnp.dot(mext_ref[...], wout_ref[...],
                                             preferred_element_type=F32)
    xext_ref[...] = x1
    lhs_ref[...] = _norm_modulate(x1, ng_ref[...], sh_ref[0], sc_ref[0]).astype(BF16)
    zeros = jnp.zeros((HALO, D_MODEL), BF16)
    if context:
        lhs_ref[top, :] = zeros
        lhs_ref[bot, :] = zeros
    else:
        seq_pos = pl.program_id(0) % blocks_per_seq
        lhs_ref[top, :] = jnp.where(seq_pos == 0, zeros, lhs_ref[top, :])
        lhs_ref[bot, :] = jnp.where(seq_pos == blocks_per_seq - 1, zeros, lhs_ref[bot, :])

    def conv(a_ref, cols):
        a = a_ref[...]
        n_ext = FFN_ROWS + 2 * HALO
        prev = pltpu.roll(a, 1, 0)[HALO:HALO + FFN_ROWS, :]
        cur = a[HALO:HALO + FFN_ROWS, :]
        nxt = pltpu.roll(a, n_ext - 1, 0)[HALO:HALO + FFN_ROWS, :]
        if context:
            prev = prev * mask_ref[0]
            nxt = nxt * mask_ref[1]
        return (prev * cw_ref[0:1, cols] + cur * cw_ref[1:2, cols] + nxt * cw_ref[2:3, cols]
                + cb_ref[:, cols])

    for t in range(N_FF_TILES):
        gcols = slice(FF_TILE * t, FF_TILE * t + FF_TILE)
        vcols = slice(D_FF + FF_TILE * t, D_FF + FF_TILE * t + FF_TILE)
        ag_ref, av_ref = a_refs[2 * (t % 2)], a_refs[2 * (t % 2) + 1]
        ag_ref[...] = jnp.dot(lhs_ref[...], wup_ref[:, gcols], preferred_element_type=F32)
        av_ref[...] = jnp.dot(lhs_ref[...], wup_ref[:, vcols], preferred_element_type=F32)
        gate = conv(ag_ref, gcols)
        val = conv(av_ref, vcols)
        y_ref[:, gcols] = ((gate / (1.0 + jnp.exp(-gate))) * val).astype(BF16)

    acc = jnp.dot(y_ref[...], wd_ref[...], preferred_element_type=F32)
    o_ref[...] = xext_ref[mid, :] + g2_ref[0] * acc


def _ffn(mix, xs, mod, norm_g, masks, w_out, w_up, conv_w, conv_b, w_down, latent_out=None):
    context = latent_out is not None
    n_lat_blocks = NLAT // FFN_ROWS
    blocks_per_seq = SEQ // FFN_ROWS
    halo_per_block = FFN_ROWS // HALO
    first = n_lat_blocks if context else 0
    n_blocks = NCTX // FFN_ROWS if context else n_lat_blocks
    ext_rows = FFN_ROWS + 2 * HALO

    row = lambda i: (first + i, 0)
    whole = lambda i: (0, 0)
    resident = dict(pipeline_mode=pl.Buffered(1))

    def cond_row(i):
        return CTX_COND_ROW if context else i // blocks_per_seq

    modspec = lambda k: pl.BlockSpec((1, 1, D_MODEL), lambda i: (cond_row(i) * 6 + k, 0, 0))
    mods = [modspec(2), modspec(3), modspec(4), modspec(5), pl.BlockSpec((1, D_MODEL), whole)]
    mod_args = (mod, mod, mod, mod, norm_g)
    weights = [
        pl.BlockSpec((MIX_WIDTH, D_MODEL), whole, **resident),
        pl.BlockSpec((D_MODEL, 2 * D_FF), whole, **resident),
        pl.BlockSpec((3, 2 * D_FF), whole, **resident),
        pl.BlockSpec((1, 2 * D_FF), whole, **resident),
        pl.BlockSpec((D_FF, D_MODEL), whole, **resident),
    ]
    weight_args = (w_out, w_up, conv_w, conv_b, w_down)
    main = pl.BlockSpec((FFN_ROWS, D_MODEL), row)
    if context:
        in_specs = [main, main] + mods + [
            pl.BlockSpec((2, FFN_ROWS, FF_TILE), lambda i: (0, 0, 0), **resident)
        ] + weights + [pl.BlockSpec(memory_space=pl.ANY)]
        args = (mix, xs) + mod_args + (masks,) + weight_args + (latent_out,)
        aliases = {len(args) - 1: 0}
        out_rows = latent_out.shape[0]
    else:
        prev_halo = pl.BlockSpec((HALO, D_MODEL),
                                 lambda i: (jnp.maximum(i * halo_per_block - 1, 0), 0))
        next_halo = pl.BlockSpec((HALO, D_MODEL),
                                 lambda i: (jnp.minimum((i + 1) * halo_per_block,
                                                        NLAT // HALO - 1), 0))
        in_specs = [main, prev_halo, next_halo, main, prev_halo, next_halo] + mods + weights
        args = (mix, mix, mix, xs, xs, xs) + mod_args + weight_args
        aliases = {}
        out_rows = mix.shape[0]
    a_shape = pltpu.VMEM((ext_rows, FF_TILE), F32)
    return pl.pallas_call(
        functools.partial(_ffn_kernel, context=context, blocks_per_seq=blocks_per_seq),
        grid=(n_blocks,),
        in_specs=in_specs,
        out_specs=pl.BlockSpec((FFN_ROWS, D_MODEL), row),
        out_shape=jax.ShapeDtypeStruct((out_rows, D_MODEL), F32),
        scratch_shapes=[pltpu.VMEM((ext_rows, MIX_WIDTH), BF16),
                        pltpu.VMEM((ext_rows, D_MODEL), F32),
                        pltpu.VMEM((ext_rows, D_MODEL), BF16),
                        a_shape, a_shape, a_shape, a_shape,
                        pltpu.VMEM((FFN_ROWS, D_FF), BF16)],
        input_output_aliases=aliases,
        compiler_params=_params(("arbitrary",), 56),
        name="ffn_ctx" if context else "ffn",
    )(*args)


def _rope_tables(head_dim):
    n_freq = head_dim // 4
    rows = SEQ // GRID_W
    row = jnp.repeat(jnp.arange(rows, dtype=F32), GRID_W)
    col = jnp.tile(jnp.arange(GRID_W, dtype=F32), rows)
    inv_freq = ROPE_BASE ** (-jnp.arange(n_freq, dtype=F32) / n_freq)
    ang = jnp.stack([row[:, None] * inv_freq, col[:, None] * inv_freq], axis=1)
    lane = np.arange(LANES)
    within = lane % head_dim
    axis = within // (2 * n_freq)
    second = (within % (2 * n_freq)) >= n_freq
    ang = ang[:, axis, lane % n_freq]
    cos = jnp.cos(ang)
    sin = jnp.where(jnp.asarray(second)[None, :], jnp.sin(ang), -jnp.sin(ang))
    cos = jnp.concatenate([cos, jnp.ones((PROJ_ROWS, LANES), F32)], axis=0)
    sin = jnp.concatenate([sin, jnp.zeros((PROJ_ROWS, LANES), F32)], axis=0)
    return cos, sin


def _group_matrix(width):
    idx = np.arange(256)
    g = (idx[:, None] // width == idx[None, :] // width).astype(np.float32) / width
    return jnp.asarray(g, dtype=BF16)


def _conv_masks():
    r = np.arange(FFN_ROWS)
    taps = (r % CTX_LEN != 0, r % CTX_LEN != CTX_LEN - 1)
    m = np.stack([np.broadcast_to(t[:, None], (FFN_ROWS, FF_TILE)) for t in taps])
    return jnp.asarray(m.astype(np.float32))


def _constants():
    cosb, sinb = _rope_tables(B_HEAD_DIM)
    cosc, sinc = _rope_tables(C_QK_DIM)
    return dict(cosb=cosb, sinb=sinb, cosc=cosc, sinc=sinc,
                g64=_group_matrix(64), g32=_group_matrix(32), masks=_conv_masks())


def kernel(x, c, ctx, c_ctx, w_ada, b_ada, norm1_g, norm2_g, w_in, a_ws, a_bs, b_qnorm, b_knorm,
           b_sink, c_qnorm, c_knorm, c_lam, c_subln, w_out, w_up, conv_w, conv_b, w_down):
    consts = _constants()
    xs = jnp.concatenate([x.reshape(NLAT, D_MODEL), ctx.reshape(NCTX, D_MODEL)], axis=0)

    cond = jnp.zeros((COND_ROWS, D_MODEL), F32).at[:BATCH].set(c).at[CTX_COND_ROW].set(c_ctx)
    mod_all = _modulation_all(cond, w_ada.astype(BF16), b_ada)
    mod_all = mod_all.reshape(DEPTH, COND_ROWS * 6, 1, D_MODEL)

    w_in_b = w_in.astype(BF16)
    w_out_b = w_out.astype(BF16)
    w_up_b = w_up.astype(BF16)
    w_down_b = w_down.astype(BF16)

    for i in range(DEPTH):
        last = i == DEPTH - 1
        lam_init = 0.8 - 0.6 * math.exp(-0.3 * i)
        mod = mod_all[i]
        zeros = jnp.zeros((4, LANES), F32)
        gains = jnp.concatenate([
            jnp.tile(b_qnorm[i] * (B_HEAD_DIM ** -0.5 * LOG2E), LANES // B_HEAD_DIM)[None],
            jnp.tile(b_knorm[i], LANES // B_HEAD_DIM)[None],
            jnp.tile(c_qnorm[i] * (C_QK_DIM ** -0.5 * LOG2E), LANES // C_QK_DIM)[None],
            jnp.tile(c_knorm[i], LANES // C_QK_DIM)[None],
            zeros], axis=0)
        u, vn, qb, kbt, vb, qc, kct, vc = _proj(
            xs, mod, norm1_g[i][None], w_in_b[i], consts, gains)

        wcat = jnp.concatenate([a_ws[i][0::2], a_ws[i][1::2]], axis=-1).astype(BF16)
        abias = jnp.repeat(a_bs[i].T, A_WIDTH // A_GROUPS, axis=1)
        subln = jnp.tile(c_subln[i] * (1.0 - lam_init), LANES // C_V_DIM)[None]
        mix = _mixers(u, vn, qb, kbt, vb, qc, kct, vc, b_sink[i], c_lam[i], wcat, abias, subln,
                      lam_init, with_ctx=not last)

        ffn_args = (mix, xs, mod, norm2_g[i][None], consts["masks"], w_out_b[i], w_up_b[i],
                    conv_w[i].reshape(3, 2 * D_FF), conv_b[i][None], w_down_b[i])
        xs = _ffn(*ffn_args)
        if not last:
            xs = _ffn(*ffn_args, latent_out=xs)

    return xs[:NLAT].reshape(BATCH, SEQ, D_MODEL)
```

```python
import functools
import math

import jax
import jax.numpy as jnp
import numpy as np
from jax import lax
from jax.experimental import pallas as pl
from jax.experimental.pallas import tpu as pltpu

F32 = jnp.float32
BF16 = jnp.bfloat16

D_MODEL = 1024
BATCH = 16
SEQ = 2048
DEPTH = 4
GRID_W = 64
CTX_LEN = 256
A_WIDTH = 256
A_GROUPS = 4
CHUNK = 128
B_HEAD_DIM = 64
B_HEADS = 8
B_KV_HEADS = 2
WINDOW = 128
C_V_DIM = 64
C_QK_DIM = 32
C_HEADS = 4
PROJ_WIDTH = 2048
MIX_WIDTH = 1024
D_FF = 2816
ROPE_BASE = 10000.0
EPS = 1e-6

LOG2E = 1.4426950408889634
NEG_BIG = -1e30

LANES = 128
BF16_SUBLANES = 16

NLAT = BATCH * SEQ
NCTX = BATCH * CTX_LEN
NTOK = NLAT + NCTX
COND_ROWS = 24
CTX_COND_ROW = BATCH

PROJ_ROWS = 512
ATT_ROWS = 256
FFN_ROWS = 512
FF_TILE = 256
N_FF_TILES = D_FF // FF_TILE
ADA_TILE = 1536

OFF_AU, OFF_AV, OFF_BQ, OFF_BK, OFF_BV, OFF_CQ, OFF_CK, OFF_CV = (
    0, 256, 512, 1024, 1152, 1280, 1536, 1792)


def _params(sem, vmem_mb):
    return pltpu.CompilerParams(dimension_semantics=sem,
                                vmem_limit_bytes=vmem_mb * 1024 * 1024)


def _mod_kernel(cond_ref, w_ref, b_ref, o_ref):
    c = cond_ref[...]
    s = c / (1.0 + jnp.exp(-c))
    o_ref[0] = jnp.dot(s.astype(BF16), w_ref[0].astype(BF16),
                       preferred_element_type=F32) + b_ref[0]


def _modulation_all(cond, w_ada, b_ada):
    n_tiles = (6 * D_MODEL) // ADA_TILE
    return pl.pallas_call(
        _mod_kernel,
        grid=(DEPTH, n_tiles),
        in_specs=[
            pl.BlockSpec((COND_ROWS, D_MODEL), lambda l, j: (0, 0)),
            pl.BlockSpec((1, D_MODEL, ADA_TILE), lambda l, j: (l, 0, j)),
            pl.BlockSpec((1, 1, ADA_TILE), lambda l, j: (l, 0, j)),
        ],
        out_specs=pl.BlockSpec((1, COND_ROWS, ADA_TILE), lambda l, j: (l, 0, j)),
        out_shape=jax.ShapeDtypeStruct((DEPTH, COND_ROWS, 6 * D_MODEL), F32),
        compiler_params=_params(("arbitrary", "arbitrary"), 32),
        name="modulation",
    )(cond, w_ada, b_ada.reshape(DEPTH, 1, 6 * D_MODEL))


def _norm_modulate(x, gain, shift, scale):
    ms = jnp.mean(x * x, axis=-1, keepdims=True)
    return (x * lax.rsqrt(ms + EPS) * gain) * (1.0 + scale) + shift


def _gelu_tanh(x):
    return 0.5 * x * (1.0 + jnp.tanh(math.sqrt(2.0 / math.pi) * (x + 0.044715 * (x * x * x))))


def _group_mean_sq(t, g):
    return jnp.dot((t * t).astype(BF16), g, preferred_element_type=F32)


def _swap_halves(t, half):
    lane = lax.broadcasted_iota(jnp.int32, t.shape, 1)
    first = (lane & (2 * half - 1)) < half
    return jnp.where(first, pltpu.roll(t, LANES - half, 1), pltpu.roll(t, half, 1))


def _lane_iota(shape):
    return lax.broadcasted_iota(jnp.int32, shape, 1)


def _proj_kernel(x_ref, sh_ref, sc_ref, ng_ref, w_ref, g64_ref, g32_ref,
                 cosb_ref, sinb_ref, cosc_ref, sinc_ref, gains_ref,
                 u_ref, vn_ref, qb_ref, kbt_ref, vb_ref, qc_ref, kct_ref, vc_ref):
    h = _norm_modulate(x_ref[...], ng_ref[...], sh_ref[0], sc_ref[0])
    z = jnp.dot(h.astype(BF16), w_ref[...], preferred_element_type=F32)

    u_ref[...] = _gelu_tanh(z[:, OFF_AU:OFF_AU + A_WIDTH])
    v = _gelu_tanh(z[:, OFF_AV:OFF_AV + A_WIDTH])
    vn_ref[...] = (v * lax.rsqrt(_group_mean_sq(v, g64_ref[...]) + EPS)).astype(BF16)

    def norm_rope(t, ms, gain, cos_ref, sin_ref, half):
        tg = t * gain
        return lax.rsqrt(ms + EPS) * (tg * cos_ref[...] + _swap_halves(tg, half) * sin_ref[...])

    for t in range(4):
        lo = OFF_BQ + 256 * (t // 2)
        if t % 2 == 0:
            q2 = z[:, lo:lo + 256]
            ms2 = _group_mean_sq(q2, g64_ref[...])
        s = slice(LANES * (t % 2), LANES * (t % 2) + LANES)
        qb_ref[:, LANES * t:LANES * t + LANES] = norm_rope(
            q2[:, s], ms2[:, s], gains_ref[0:1, :], cosb_ref, sinb_ref, 16).astype(BF16)

    kb = z[:, OFF_BK:OFF_BK + LANES]
    kb = norm_rope(kb, _group_mean_sq(kb, g64_ref[0:LANES, 0:LANES]),
                   gains_ref[1:2, :], cosb_ref, sinb_ref, 16)
    kbt = kb.T.astype(BF16)
    kbt_ref[0:LANES, :] = kbt
    kbt_ref[LANES:LANES + 64, :] = kbt[64:LANES, :]
    kbt_ref[LANES + 64:2 * LANES, :] = kbt[0:64, :]

    vb = z[:, OFF_BV:OFF_BV + LANES]
    ones = jnp.ones((PROJ_ROWS, LANES), BF16)
    vb_ref[:, 0:LANES] = vb.astype(BF16)
    vb_ref[:, LANES:2 * LANES] = ones
    vb_ref[:, 2 * LANES:3 * LANES] = pltpu.roll(vb, 64, 1).astype(BF16)
    vb_ref[:, 3 * LANES:4 * LANES] = ones

    cq = z[:, OFF_CQ:OFF_CQ + 256]
    msq = _group_mean_sq(cq, g32_ref[...])
    ck = z[:, OFF_CK:OFF_CK + 256]
    msk = _group_mean_sq(ck, g32_ref[...])
    for t in range(2):
        s = slice(LANES * t, LANES * t + LANES)
        qc_ref[:, s] = norm_rope(cq[:, s], msq[:, s], gains_ref[2:3, :],
                                 cosc_ref, sinc_ref, 8).astype(BF16)
        kc = norm_rope(ck[:, s], msk[:, s], gains_ref[3:4, :], cosc_ref, sinc_ref, 8)
        kct_ref[s, :] = kc.T.astype(BF16)

    cv = z[:, OFF_CV:OFF_CV + 256]
    vc_ref[:, 0:LANES] = cv[:, 0:LANES].astype(BF16)
    vc_ref[:, LANES:2 * LANES] = ones
    vc_ref[:, 2 * LANES:3 * LANES] = cv[:, LANES:2 * LANES].astype(BF16)
    vc_ref[:, 3 * LANES:4 * LANES] = ones


def _proj(xs, mod, norm_g, w_in, consts, gains):
    n_lat_blocks = NLAT // PROJ_ROWS
    blocks_per_seq = SEQ // PROJ_ROWS

    def cond_row(i):
        return jnp.where(i < n_lat_blocks, i // blocks_per_seq, CTX_COND_ROW)

    def pos_block(i):
        return jnp.where(i < n_lat_blocks, i % blocks_per_seq, blocks_per_seq)

    row = lambda i: (i, 0)
    col = lambda i: (0, i)
    whole = lambda i: (0, 0)
    pos = lambda i: (pos_block(i), 0)
    out_shape = (
        jax.ShapeDtypeStruct((NTOK, 256), F32),
        jax.ShapeDtypeStruct((NTOK, 256), BF16),
        jax.ShapeDtypeStruct((NTOK, 512), BF16),
        jax.ShapeDtypeStruct((256, NTOK), BF16),
        jax.ShapeDtypeStruct((NTOK, 512), BF16),
        jax.ShapeDtypeStruct((NTOK, 256), BF16),
        jax.ShapeDtypeStruct((256, NTOK), BF16),
        jax.ShapeDtypeStruct((NTOK, 512), BF16),
    )
    out_specs = (
        pl.BlockSpec((PROJ_ROWS, 256), row),
        pl.BlockSpec((PROJ_ROWS, 256), row),
        pl.BlockSpec((PROJ_ROWS, 512), row),
        pl.BlockSpec((256, PROJ_ROWS), col),
        pl.BlockSpec((PROJ_ROWS, 512), row),
        pl.BlockSpec((PROJ_ROWS, 256), row),
        pl.BlockSpec((256, PROJ_ROWS), col),
        pl.BlockSpec((PROJ_ROWS, 512), row),
    )
    return pl.pallas_call(
        _proj_kernel,
        grid=(NTOK // PROJ_ROWS,),
        in_specs=[
            pl.BlockSpec((PROJ_ROWS, D_MODEL), row),
            pl.BlockSpec((1, 1, D_MODEL), lambda i: (cond_row(i) * 6 + 0, 0, 0)),
            pl.BlockSpec((1, 1, D_MODEL), lambda i: (cond_row(i) * 6 + 1, 0, 0)),
            pl.BlockSpec((1, D_MODEL), whole),
            pl.BlockSpec((D_MODEL, PROJ_WIDTH), whole),
            pl.BlockSpec((256, 256), whole),
            pl.BlockSpec((256, 256), whole),
            pl.BlockSpec((PROJ_ROWS, LANES), pos),
            pl.BlockSpec((PROJ_ROWS, LANES), pos),
            pl.BlockSpec((PROJ_ROWS, LANES), pos),
            pl.BlockSpec((PROJ_ROWS, LANES), pos),
            pl.BlockSpec((8, LANES), whole),
        ],
        out_specs=out_specs,
        out_shape=out_shape,
        compiler_params=_params(("arbitrary",), 48),
        name="proj",
    )(xs, mod, mod, norm_g, w_in, consts["g64"], consts["g32"],
      consts["cosb"], consts["sinb"], consts["cosc"], consts["sinc"], gains)


def _exp2_bf16(s, m):
    return jnp.exp2(s - m).astype(BF16)


def _row_max(s):
    return jnp.max(s, axis=-1, keepdims=True)


def _mixer_kernel(u_ref, vn_ref, qb_ref, qc_ref,
                  kbl_ref, kbc_ref, vbl_ref, vbc_ref,
                  kcl_ref, kcc_ref, vcl_ref, vcc_ref,
                  sink_ref, lam_ref, wcat_ref, abias_ref, subln_ref,
                  mix_ref, *, lam_init, n_lat_qblocks):
    n = pl.program_id(1)
    lane = _lane_iota((ATT_ROWS, LANES))
    low_half = lane < 64
    dot = functools.partial(jnp.dot, preferred_element_type=F32)

    def mixer_a():
        for c in range(ATT_ROWS // CHUNK):
            rows = slice(CHUNK * c, CHUNK * c + CHUNK)
            for t in range(2):
                cols = slice(LANES * t, LANES * t + LANES)
                vt = vn_ref[rows, cols].astype(F32)
                keep_lo = _lane_iota(vt.shape) < 64
                rhs = jnp.concatenate([jnp.where(keep_lo, vt, 0.0), jnp.where(keep_lo, 0.0, vt)],
                                      axis=0).astype(BF16)
                mixed = dot(wcat_ref[t], rhs) + abias_ref[:, cols]
                mix_ref[rows, cols] = (u_ref[rows, cols] * mixed).astype(BF16)

    lp = lam_ref[...]
    lam = (jnp.exp(jnp.sum(lp[0:1] * lp[1:2], axis=-1, keepdims=True))
           - jnp.exp(jnp.sum(lp[2:3] * lp[3:4], axis=-1, keepdims=True)) + lam_init)

    def head_mask(offset, width):
        return jnp.where((lane >= offset) & (lane < offset + width), 1.0, 0.0)

    def softmax_pv(scores, values, sink2):
        m = _row_max(scores[0])
        for s in scores[1:]:
            m = jnp.maximum(m, _row_max(s))
        if sink2 is not None:
            m = jnp.maximum(m, sink2)
        oa = dot(_exp2_bf16(scores[0], m), values[0])
        for s, v in zip(scores[1:], values[1:]):
            oa = oa + dot(_exp2_bf16(s, m), v)
        den = oa[:, LANES:2 * LANES]
        if sink2 is not None:
            den = den + jnp.exp2(sink2 - m)
        return oa[:, 0:LANES] / den

    def mixer_b(latent):
        if latent:
            start = pl.multiple_of(jnp.clip(n * ATT_ROWS - WINDOW, 0, SEQ - 2 * ATT_ROWS), LANES)
            qpos = n * ATT_ROWS + lax.broadcasted_iota(jnp.int32, (ATT_ROWS, 2 * ATT_ROWS), 0)
            kpos = start + lax.broadcasted_iota(jnp.int32, (ATT_ROWS, 2 * ATT_ROWS), 1)
            valid = jnp.abs(qpos - kpos) <= WINDOW
        for p in range(B_HEADS // 2):
            qt = qb_ref[:, LANES * p:LANES * p + LANES].astype(F32)
            outs = []
            for r in range(2):
                h = 2 * p + r
                j = h // (B_HEADS // B_KV_HEADS)
                var = 0 if j == r else 1
                krows = slice(LANES * var, LANES * var + LANES)
                vcols = slice(256 * var, 256 * var + 256)
                qm = (qt * head_mask(64 * r, 64)).astype(BF16)
                sink2 = sink_ref[h] * LOG2E
                s_ctx = dot(qm, kbc_ref[krows, :])
                if latent:
                    s_win = dot(qm, kbl_ref[krows, pl.ds(start, 2 * ATT_ROWS)])
                    s_win = jnp.where(valid, s_win, NEG_BIG)
                    outs.append(softmax_pv([s_win, s_ctx],
                                           [vbl_ref[pl.ds(start, 2 * ATT_ROWS), vcols],
                                            vbc_ref[:, vcols]], sink2))
                else:
                    outs.append(softmax_pv([s_ctx], [vbc_ref[:, vcols]], sink2))
            mix_ref[:, A_WIDTH + LANES * p:A_WIDTH + LANES * p + LANES] = (
                jnp.where(low_half, outs[0], outs[1]).astype(BF16))

    def mixer_c(latent):
        for t in range(C_HEADS // 2):
            qt = qc_ref[:, LANES * t:LANES * t + LANES].astype(F32)
            krows = slice(LANES * t, LANES * t + LANES)
            vcols = slice(256 * t, 256 * t + 256)
            outs = []
            for r in range(2):
                maps = []
                for mp in range(2):
                    qm = (qt * head_mask(64 * r + C_QK_DIM * mp, C_QK_DIM)).astype(BF16)
                    s_ctx = dot(qm, kcc_ref[krows, :])
                    if latent:
                        s_lat = dot(qm, kcl_ref[krows, :])
                        maps.append(softmax_pv([s_lat, s_ctx],
                                               [vcl_ref[:, vcols], vcc_ref[:, vcols]], None))
                    else:
                        maps.append(softmax_pv([s_ctx], [vcc_ref[:, vcols]], None))
                outs.append(maps[0] - lam * maps[1])
            o = jnp.where(low_half, outs[0], outs[1])
            o2 = o * o
            ms_lo = jnp.sum(jnp.where(low_half, o2, 0.0), axis=-1, keepdims=True) * (1.0 / C_V_DIM)
            ms_hi = jnp.sum(jnp.where(low_half, 0.0, o2), axis=-1, keepdims=True) * (1.0 / C_V_DIM)
            rs = jnp.where(low_half, lax.rsqrt(ms_lo + EPS), lax.rsqrt(ms_hi + EPS))
            off = A_WIDTH + 512 + LANES * t
            mix_ref[:, off:off + LANES] = (o * rs * subln_ref[...]).astype(BF16)

    @pl.when(n < n_lat_qblocks)
    def _():
        mixer_c(True)
        mixer_b(True)
        mixer_a()

    @pl.when(n >= n_lat_qblocks)
    def _():
        mixer_c(False)
        mixer_b(False)
        mixer_a()


def _mixers(u, vn, qb, kbt, vb, qc, kct, vc, sink, c_lam, wcat, abias, subln, lam_init, with_ctx):
    n_lat_qblocks = SEQ // ATT_ROWS
    n_qblocks = n_lat_qblocks + (1 if with_ctx else 0)
    lat_q_total = NLAT // ATT_ROWS
    out_rows = NTOK if with_ctx else NLAT

    def qrow(b, n):
        return (jnp.where(n < n_lat_qblocks, b * n_lat_qblocks + n, lat_q_total + b), 0)

    lat_rows = lambda b, n: (b, 0)
    ctx_rows = lambda b, n: (NLAT // CTX_LEN + b, 0)
    lat_cols = lambda b, n: (0, b)
    ctx_cols = lambda b, n: (0, NLAT // CTX_LEN + b)
    whole2 = lambda b, n: (0, 0)
    kernel = functools.partial(_mixer_kernel, lam_init=lam_init, n_lat_qblocks=n_lat_qblocks)
    return pl.pallas_call(
        kernel,
        grid=(BATCH, n_qblocks),
        in_specs=[
            pl.BlockSpec((ATT_ROWS, 256), qrow),
            pl.BlockSpec((ATT_ROWS, 256), qrow),
            pl.BlockSpec((ATT_ROWS, 512), qrow),
            pl.BlockSpec((ATT_ROWS, 256), qrow),
            pl.BlockSpec((256, SEQ), lat_cols),
            pl.BlockSpec((256, CTX_LEN), ctx_cols),
            pl.BlockSpec((SEQ, 512), lat_rows),
            pl.BlockSpec((CTX_LEN, 512), ctx_rows),
            pl.BlockSpec((256, SEQ), lat_cols),
            pl.BlockSpec((256, CTX_LEN), ctx_cols),
            pl.BlockSpec((SEQ, 512), lat_rows),
            pl.BlockSpec((CTX_LEN, 512), ctx_rows),
            pl.BlockSpec(memory_space=pltpu.SMEM),
            pl.BlockSpec((4, C_QK_DIM), whole2),
            pl.BlockSpec((2, CHUNK, 256), lambda b, n: (0, 0, 0)),
            pl.BlockSpec((CHUNK, 256), whole2),
            pl.BlockSpec((1, LANES), whole2),
        ],
        out_specs=pl.BlockSpec((ATT_ROWS, MIX_WIDTH), qrow),
        out_shape=jax.ShapeDtypeStruct((out_rows, MIX_WIDTH), BF16),
        compiler_params=_params(("arbitrary", "arbitrary"), 48),
        name="mixers",
    )(u, vn, qb, qc, kbt, kbt, vb, vb, kct, kct, vc, vc, sink, c_lam, wcat, abias, subln)


HALO = BF16_SUBLANES


SEAM_PAD = 8


def _ffn_kernel(mix_ref, mixp_ref, mixn_ref, x_ref, xp_ref, xn_ref,
                g1_ref, sh_ref, sc_ref, g2_ref, ng_ref,
                wout_ref, wup_ref, cw_ref, cb_ref, wd_ref,
                o_ref, mext_ref, xext_ref, lhs_ref, *scratch, n_lat_blocks, blocks_per_seq):
    *a_refs, y_ref = scratch
    top = slice(0, HALO)
    mid = slice(HALO, HALO + FFN_ROWS)
    bot = slice(HALO + FFN_ROWS, 2 * HALO + FFN_ROWS)
    i = pl.program_id(0)
    is_ctx = i >= n_lat_blocks
    seq_pos = i % blocks_per_seq
    no_prev = jnp.logical_or(is_ctx, seq_pos == 0)
    no_next = jnp.logical_or(is_ctx, seq_pos == blocks_per_seq - 1)

    mext_ref[top, :] = mixp_ref[...]
    mext_ref[mid, :] = mix_ref[...]
    mext_ref[bot, :] = mixn_ref[...]
    xext_ref[top, :] = xp_ref[...]
    xext_ref[mid, :] = x_ref[...]
    xext_ref[bot, :] = xn_ref[...]

    x1 = xext_ref[...] + g1_ref[0] * jnp.dot(mext_ref[...], wout_ref[...],
                                             preferred_element_type=F32)
    xext_ref[...] = x1
    lhs_ref[...] = _norm_modulate(x1, ng_ref[...], sh_ref[0], sc_ref[0]).astype(BF16)
    zeros = jnp.zeros((HALO, D_MODEL), BF16)
    lhs_ref[top, :] = jnp.where(no_prev, zeros, lhs_ref[top, :])
    lhs_ref[bot, :] = jnp.where(no_next, zeros, lhs_ref[bot, :])

    def conv(a, pad, cols, mprev=None, mnext=None):
        n = a.shape[0]
        rows = slice(pad, n - pad)
        prev = pltpu.roll(a, 1, 0)[rows, :]
        nxt = pltpu.roll(a, n - 1, 0)[rows, :]
        if mprev is not None:
            prev = prev * mprev
            nxt = nxt * mnext
        return (prev * cw_ref[0:1, cols] + a[rows, :] * cw_ref[1:2, cols] + nxt * cw_ref[2:3, cols]
                + cb_ref[:, cols])

    def activate(gate, val):
        return ((gate / (1.0 + jnp.exp(-gate))) * val).astype(BF16)

    seams = []
    for seam in range(CTX_LEN, FFN_ROWS, CTX_LEN):
        lo = seam - BF16_SUBLANES
        row = lo + lax.broadcasted_iota(jnp.int32, (2 * BF16_SUBLANES, FF_TILE), 0)
        mprev = jnp.where(jnp.logical_and(is_ctx, row == seam), 0.0, 1.0)
        mnext = jnp.where(jnp.logical_and(is_ctx, row == seam - 1), 0.0, 1.0)
        seams.append((lo, mprev, mnext))

    for t in range(N_FF_TILES):
        gcols = slice(FF_TILE * t, FF_TILE * t + FF_TILE)
        vcols = slice(D_FF + FF_TILE * t, D_FF + FF_TILE * t + FF_TILE)
        ag_ref, av_ref = a_refs[2 * (t % 2)], a_refs[2 * (t % 2) + 1]
        ag_ref[...] = jnp.dot(lhs_ref[...], wup_ref[:, gcols], preferred_element_type=F32)
        av_ref[...] = jnp.dot(lhs_ref[...], wup_ref[:, vcols], preferred_element_type=F32)
        y_ref[:, gcols] = activate(conv(ag_ref[...], HALO, gcols), conv(av_ref[...], HALO, vcols))
        for lo, mprev, mnext in seams:
            slab = slice(HALO + lo - SEAM_PAD, HALO + lo + 2 * BF16_SUBLANES + SEAM_PAD)
            y_ref[lo:lo + 2 * BF16_SUBLANES, gcols] = activate(
                conv(ag_ref[slab, :], SEAM_PAD, gcols, mprev, mnext),
                conv(av_ref[slab, :], SEAM_PAD, vcols, mprev, mnext))

    acc = jnp.dot(y_ref[...], wd_ref[...], preferred_element_type=F32)
    o_ref[...] = xext_ref[mid, :] + g2_ref[0] * acc


def _ffn(mix, xs, mod, norm_g, w_out, w_up, conv_w, conv_b, w_down):
    n_rows = mix.shape[0]
    n_lat_blocks = NLAT // FFN_ROWS
    blocks_per_seq = SEQ // FFN_ROWS
    halo_per_block = FFN_ROWS // HALO
    ext_rows = FFN_ROWS + 2 * HALO

    row = lambda i: (i, 0)
    whole = lambda i: (0, 0)
    resident = dict(pipeline_mode=pl.Buffered(1))

    def cond_row(i):
        return jnp.where(i < n_lat_blocks, i // blocks_per_seq, CTX_COND_ROW)

    modspec = lambda k: pl.BlockSpec((1, 1, D_MODEL), lambda i: (cond_row(i) * 6 + k, 0, 0))
    main = pl.BlockSpec((FFN_ROWS, D_MODEL), row)
    prev_halo = pl.BlockSpec((HALO, D_MODEL),
                             lambda i: (jnp.maximum(i * halo_per_block - 1, 0), 0))
    next_halo = pl.BlockSpec((HALO, D_MODEL),
                             lambda i: (jnp.minimum((i + 1) * halo_per_block,
                                                    n_rows // HALO - 1), 0))
    in_specs = [
        main, prev_halo, next_halo, main, prev_halo, next_halo,
        modspec(2), modspec(3), modspec(4), modspec(5), pl.BlockSpec((1, D_MODEL), whole),
        pl.BlockSpec((MIX_WIDTH, D_MODEL), whole, **resident),
        pl.BlockSpec((D_MODEL, 2 * D_FF), whole, **resident),
        pl.BlockSpec((3, 2 * D_FF), whole, **resident),
        pl.BlockSpec((1, 2 * D_FF), whole, **resident),
        pl.BlockSpec((D_FF, D_MODEL), whole, **resident),
    ]
    a_shape = pltpu.VMEM((ext_rows, FF_TILE), F32)
    return pl.pallas_call(
        functools.partial(_ffn_kernel, n_lat_blocks=n_lat_blocks, blocks_per_seq=blocks_per_seq),
        grid=(n_rows // FFN_ROWS,),
        in_specs=in_specs,
        out_specs=main,
        out_shape=jax.ShapeDtypeStruct((n_rows, D_MODEL), F32),
        scratch_shapes=[pltpu.VMEM((ext_rows, MIX_WIDTH), BF16),
                        pltpu.VMEM((ext_rows, D_MODEL), F32),
                        pltpu.VMEM((ext_rows, D_MODEL), BF16),
                        a_shape, a_shape, a_shape, a_shape,
                        pltpu.VMEM((FFN_ROWS, D_FF), BF16)],
        compiler_params=_params(("arbitrary",), 56),
        name="ffn",
    )(mix, mix, mix, xs, xs, xs, mod, mod, mod, mod, norm_g, w_out, w_up, conv_w, conv_b, w_down)


def _rope_tables(head_dim):
    n_freq = head_dim // 4
    rows = SEQ // GRID_W
    row = jnp.repeat(jnp.arange(rows, dtype=F32), GRID_W)
    col = jnp.tile(jnp.arange(GRID_W, dtype=F32), rows)
    inv_freq = ROPE_BASE ** (-jnp.arange(n_freq, dtype=F32) / n_freq)
    ang = jnp.stack([row[:, None] * inv_freq, col[:, None] * inv_freq], axis=1)
    lane = np.arange(LANES)
    within = lane % head_dim
    axis = within // (2 * n_freq)
    second = (within % (2 * n_freq)) >= n_freq
    ang = ang[:, axis, lane % n_freq]
    cos = jnp.cos(ang)
    sin = jnp.where(jnp.asarray(second)[None, :], jnp.sin(ang), -jnp.sin(ang))
    cos = jnp.concatenate([cos, jnp.ones((PROJ_ROWS, LANES), F32)], axis=0)
    sin = jnp.concatenate([sin, jnp.zeros((PROJ_ROWS, LANES), F32)], axis=0)
    return cos, sin


def _group_matrix(width):
    idx = np.arange(256)
    g = (idx[:, None] // width == idx[None, :] // width).astype(np.float32) / width
    return jnp.asarray(g, dtype=BF16)


def _constants():
    cosb, sinb = _rope_tables(B_HEAD_DIM)
    cosc, sinc = _rope_tables(C_QK_DIM)
    return dict(cosb=cosb, sinb=sinb, cosc=cosc, sinc=sinc,
                g64=_group_matrix(64), g32=_group_matrix(32))


def kernel(x, c, ctx, c_ctx, w_ada, b_ada, norm1_g, norm2_g, w_in, a_ws, a_bs, b_qnorm, b_knorm,
           b_sink, c_qnorm, c_knorm, c_lam, c_subln, w_out, w_up, conv_w, conv_b, w_down):
    consts = _constants()
    xs = jnp.concatenate([x.reshape(NLAT, D_MODEL), ctx.reshape(NCTX, D_MODEL)], axis=0)

    cond = jnp.zeros((COND_ROWS, D_MODEL), F32).at[:BATCH].set(c).at[CTX_COND_ROW].set(c_ctx)
    mod_all = _modulation_all(cond, w_ada, b_ada)
    mod_all = mod_all.reshape(DEPTH, COND_ROWS * 6, 1, D_MODEL)

    w_in_b = w_in.astype(BF16)
    w_out_b = w_out.astype(BF16)
    w_up_b = w_up.astype(BF16)
    w_down_b = w_down.astype(BF16)

    for i in range(DEPTH):
        last = i == DEPTH - 1
        lam_init = 0.8 - 0.6 * math.exp(-0.3 * i)
        mod = mod_all[i]
        zeros = jnp.zeros((4, LANES), F32)
        gains = jnp.concatenate([
            jnp.tile(b_qnorm[i] * (B_HEAD_DIM ** -0.5 * LOG2E), LANES // B_HEAD_DIM)[None],
            jnp.tile(b_knorm[i], LANES // B_HEAD_DIM)[None],
            jnp.tile(c_qnorm[i] * (C_QK_DIM ** -0.5 * LOG2E), LANES // C_QK_DIM)[None],
            jnp.tile(c_knorm[i], LANES // C_QK_DIM)[None],
            zeros], axis=0)
        u, vn, qb, kbt, vb, qc, kct, vc = _proj(
            xs, mod, norm1_g[i][None], w_in_b[i], consts, gains)

        wcat = jnp.concatenate([a_ws[i][0::2], a_ws[i][1::2]], axis=-1).astype(BF16)
        abias = jnp.repeat(a_bs[i].T, A_WIDTH // A_GROUPS, axis=1)
        subln = jnp.tile(c_subln[i] * (1.0 - lam_init), LANES // C_V_DIM)[None]
        mix = _mixers(u, vn, qb, kbt, vb, qc, kct, vc, b_sink[i], c_lam[i], wcat, abias, subln,
                      lam_init, with_ctx=not last)

        xs = _ffn(mix, xs, mod, norm2_g[i][None], w_out_b[i], w_up_b[i],
                  conv_w[i].reshape(3, 2 * D_FF), conv_b[i][None], w_down_b[i])

    return xs.reshape(BATCH, SEQ, D_MODEL)
```

```python
import functools
import math

import jax
import jax.numpy as jnp
import numpy as np
from jax import lax
from jax.experimental import pallas as pl
from jax.experimental.pallas import tpu as pltpu

F32 = jnp.float32
BF16 = jnp.bfloat16

D_MODEL = 1024
BATCH = 16
SEQ = 2048
DEPTH = 4
GRID_W = 64
CTX_LEN = 256
A_WIDTH = 256
A_GROUPS = 4
CHUNK = 128
B_HEAD_DIM = 64
B_HEADS = 8
B_KV_HEADS = 2
WINDOW = 128
C_V_DIM = 64
C_QK_DIM = 32
C_HEADS = 4
PROJ_WIDTH = 2048
MIX_WIDTH = 1024
D_FF = 2816
ROPE_BASE = 10000.0
EPS = 1e-6

LOG2E = 1.4426950408889634
NEG_BIG = -1e30

LANES = 128
BF16_SUBLANES = 16

NLAT = BATCH * SEQ
NCTX = BATCH * CTX_LEN
NTOK = NLAT + NCTX
COND_ROWS = 24
CTX_COND_ROW = BATCH

PROJ_ROWS = 1024
ATT_ROWS = 256
FFN_ROWS = 512
FF_TILE = 256
N_FF_TILES = D_FF // FF_TILE
ADA_TILE = 1536

OFF_AU, OFF_AV, OFF_BQ, OFF_BK, OFF_BV, OFF_CQ, OFF_CK, OFF_CV = (
    0, 256, 512, 1024, 1152, 1280, 1536, 1792)


def _params(sem, vmem_mb):
    return pltpu.CompilerParams(dimension_semantics=sem,
                                vmem_limit_bytes=vmem_mb * 1024 * 1024)


def _mod_kernel(cond_ref, w_ref, b_ref, o_ref):
    c = cond_ref[...]
    s = c / (1.0 + jnp.exp(-c))
    o_ref[0] = jnp.dot(s.astype(BF16), w_ref[0].astype(BF16),
                       preferred_element_type=F32) + b_ref[0]


def _modulation_all(cond, w_ada, b_ada):
    n_tiles = (6 * D_MODEL) // ADA_TILE
    return pl.pallas_call(
        _mod_kernel,
        grid=(DEPTH, n_tiles),
        in_specs=[
            pl.BlockSpec((COND_ROWS, D_MODEL), lambda l, j: (0, 0)),
            pl.BlockSpec((1, D_MODEL, ADA_TILE), lambda l, j: (l, 0, j)),
            pl.BlockSpec((1, 1, ADA_TILE), lambda l, j: (l, 0, j)),
        ],
        out_specs=pl.BlockSpec((1, COND_ROWS, ADA_TILE), lambda l, j: (l, 0, j)),
        out_shape=jax.ShapeDtypeStruct((DEPTH, COND_ROWS, 6 * D_MODEL), F32),
        compiler_params=_params(("arbitrary", "arbitrary"), 32),
        name="modulation",
    )(cond, w_ada, b_ada.reshape(DEPTH, 1, 6 * D_MODEL))


def _norm_modulate(x, gain, shift, scale):
    ms = jnp.mean(x * x, axis=-1, keepdims=True)
    return (x * lax.rsqrt(ms + EPS) * gain) * (1.0 + scale) + shift


def _gelu_tanh(x):
    return 0.5 * x * (1.0 + jnp.tanh(math.sqrt(2.0 / math.pi) * (x + 0.044715 * (x * x * x))))


def _group_mean_sq(t, g):
    return jnp.dot((t * t).astype(BF16), g, preferred_element_type=F32)


def _swap_halves(t, half):
    lane = lax.broadcasted_iota(jnp.int32, t.shape, 1)
    first = (lane & (2 * half - 1)) < half
    return jnp.where(first, pltpu.roll(t, LANES - half, 1), pltpu.roll(t, half, 1))


def _lane_iota(shape):
    return lax.broadcasted_iota(jnp.int32, shape, 1)


def _proj_kernel(x_ref, sh_ref, sc_ref, ng_ref, w_ref, g64_ref, g32_ref,
                 cosb_ref, sinb_ref, cosc_ref, sinc_ref, gains_ref,
                 u_ref, vn_ref, qb_ref, kbt_ref, vb_ref, qc_ref, kct_ref, vc_ref):
    h = _norm_modulate(x_ref[...], ng_ref[...], sh_ref[0], sc_ref[0])
    z = jnp.dot(h.astype(BF16), w_ref[...], preferred_element_type=F32)

    u_ref[...] = _gelu_tanh(z[:, OFF_AU:OFF_AU + A_WIDTH])
    v = _gelu_tanh(z[:, OFF_AV:OFF_AV + A_WIDTH])
    vn_ref[...] = (v * lax.rsqrt(_group_mean_sq(v, g64_ref[...]) + EPS)).astype(BF16)

    def norm_rope(t, ms, gain, cos_ref, sin_ref, half):
        tg = t * gain
        return lax.rsqrt(ms + EPS) * (tg * cos_ref[...] + _swap_halves(tg, half) * sin_ref[...])

    for t in range(4):
        lo = OFF_BQ + 256 * (t // 2)
        if t % 2 == 0:
            q2 = z[:, lo:lo + 256]
            ms2 = _group_mean_sq(q2, g64_ref[...])
        s = slice(LANES * (t % 2), LANES * (t % 2) + LANES)
        qb_ref[:, LANES * t:LANES * t + LANES] = norm_rope(
            q2[:, s], ms2[:, s], gains_ref[0:1, :], cosb_ref, sinb_ref, 16).astype(BF16)

    kb = z[:, OFF_BK:OFF_BK + LANES]
    kb = norm_rope(kb, _group_mean_sq(kb, g64_ref[0:LANES, 0:LANES]),
                   gains_ref[1:2, :], cosb_ref, sinb_ref, 16)
    kbt = kb.T.astype(BF16)
    kbt_ref[0:LANES, :] = kbt
    kbt_ref[LANES:LANES + 64, :] = kbt[64:LANES, :]
    kbt_ref[LANES + 64:2 * LANES, :] = kbt[0:64, :]

    vb = z[:, OFF_BV:OFF_BV + LANES]
    ones = jnp.ones((PROJ_ROWS, LANES), BF16)
    vb_ref[:, 0:LANES] = vb.astype(BF16)
    vb_ref[:, LANES:2 * LANES] = ones
    vb_ref[:, 2 * LANES:3 * LANES] = pltpu.roll(vb, 64, 1).astype(BF16)
    vb_ref[:, 3 * LANES:4 * LANES] = ones

    cq = z[:, OFF_CQ:OFF_CQ + 256]
    msq = _group_mean_sq(cq, g32_ref[...])
    ck = z[:, OFF_CK:OFF_CK + 256]
    msk = _group_mean_sq(ck, g32_ref[...])
    for t in range(2):
        s = slice(LANES * t, LANES * t + LANES)
        qc_ref[:, s] = norm_rope(cq[:, s], msq[:, s], gains_ref[2:3, :],
                                 cosc_ref, sinc_ref, 8).astype(BF16)
        kc = norm_rope(ck[:, s], msk[:, s], gains_ref[3:4, :], cosc_ref, sinc_ref, 8)
        kct_ref[s, :] = kc.T.astype(BF16)

    cv = z[:, OFF_CV:OFF_CV + 256]
    vc_ref[:, 0:LANES] = cv[:, 0:LANES].astype(BF16)
    vc_ref[:, LANES:2 * LANES] = ones
    vc_ref[:, 2 * LANES:3 * LANES] = cv[:, LANES:2 * LANES].astype(BF16)
    vc_ref[:, 3 * LANES:4 * LANES] = ones


def _proj(xs, mod, norm_g, w_in, consts, gains):
    n_lat_blocks = NLAT // PROJ_ROWS
    blocks_per_seq = SEQ // PROJ_ROWS

    def cond_row(i):
        return jnp.where(i < n_lat_blocks, i // blocks_per_seq, CTX_COND_ROW)

    def pos_block(i):
        return jnp.where(i < n_lat_blocks, i % blocks_per_seq, blocks_per_seq)

    row = lambda i: (i, 0)
    col = lambda i: (0, i)
    whole = lambda i: (0, 0)
    pos = lambda i: (pos_block(i), 0)
    out_shape = (
        jax.ShapeDtypeStruct((NTOK, 256), F32),
        jax.ShapeDtypeStruct((NTOK, 256), BF16),
        jax.ShapeDtypeStruct((NTOK, 512), BF16),
        jax.ShapeDtypeStruct((256, NTOK), BF16),
        jax.ShapeDtypeStruct((NTOK, 512), BF16),
        jax.ShapeDtypeStruct((NTOK, 256), BF16),
        jax.ShapeDtypeStruct((256, NTOK), BF16),
        jax.ShapeDtypeStruct((NTOK, 512), BF16),
    )
    out_specs = (
        pl.BlockSpec((PROJ_ROWS, 256), row),
        pl.BlockSpec((PROJ_ROWS, 256), row),
        pl.BlockSpec((PROJ_ROWS, 512), row),
        pl.BlockSpec((256, PROJ_ROWS), col),
        pl.BlockSpec((PROJ_ROWS, 512), row),
        pl.BlockSpec((PROJ_ROWS, 256), row),
        pl.BlockSpec((256, PROJ_ROWS), col),
        pl.BlockSpec((PROJ_ROWS, 512), row),
    )
    return pl.pallas_call(
        _proj_kernel,
        grid=(NTOK // PROJ_ROWS,),
        in_specs=[
            pl.BlockSpec((PROJ_ROWS, D_MODEL), row),
            pl.BlockSpec((1, 1, D_MODEL), lambda i: (cond_row(i) * 6 + 0, 0, 0)),
            pl.BlockSpec((1, 1, D_MODEL), lambda i: (cond_row(i) * 6 + 1, 0, 0)),
            pl.BlockSpec((1, D_MODEL), whole),
            pl.BlockSpec((D_MODEL, PROJ_WIDTH), whole),
            pl.BlockSpec((256, 256), whole),
            pl.BlockSpec((256, 256), whole),
            pl.BlockSpec((PROJ_ROWS, LANES), pos),
            pl.BlockSpec((PROJ_ROWS, LANES), pos),
            pl.BlockSpec((PROJ_ROWS, LANES), pos),
            pl.BlockSpec((PROJ_ROWS, LANES), pos),
            pl.BlockSpec((8, LANES), whole),
        ],
        out_specs=out_specs,
        out_shape=out_shape,
        compiler_params=_params(("arbitrary",), 48),
        name="proj",
    )(xs, mod, mod, norm_g, w_in, consts["g64"], consts["g32"],
      consts["cosb"], consts["sinb"], consts["cosc"], consts["sinc"], gains)


def _exp2_bf16(s, m):
    return jnp.exp2(s - m).astype(BF16)


def _row_max(s):
    return jnp.max(s, axis=-1, keepdims=True)


def _mixer_kernel(u_ref, vn_ref, qb_ref, qc_ref,
                  kbl_ref, kbc_ref, vbl_ref, vbc_ref,
                  kcl_ref, kcc_ref, vcl_ref, vcc_ref,
                  sink_ref, lam_ref, wcat_ref, abias_ref, subln_ref,
                  mix_ref, *, lam_init, n_lat_qblocks):
    n = pl.program_id(1)
    lane = _lane_iota((ATT_ROWS, LANES))
    low_half = lane < 64
    dot = functools.partial(jnp.dot, preferred_element_type=F32)

    def mixer_a():
        for c in range(ATT_ROWS // CHUNK):
            rows = slice(CHUNK * c, CHUNK * c + CHUNK)
            for t in range(2):
                cols = slice(LANES * t, LANES * t + LANES)
                vt = vn_ref[rows, cols].astype(F32)
                keep_lo = _lane_iota(vt.shape) < 64
                rhs = jnp.concatenate([jnp.where(keep_lo, vt, 0.0), jnp.where(keep_lo, 0.0, vt)],
                                      axis=0).astype(BF16)
                mixed = dot(wcat_ref[t], rhs) + abias_ref[:, cols]
                mix_ref[rows, cols] = (u_ref[rows, cols] * mixed).astype(BF16)

    lp = lam_ref[...]
    lam = (jnp.exp(jnp.sum(lp[0:1] * lp[1:2], axis=-1, keepdims=True))
           - jnp.exp(jnp.sum(lp[2:3] * lp[3:4], axis=-1, keepdims=True)) + lam_init)

    def head_mask(offset, width):
        return jnp.where((lane >= offset) & (lane < offset + width), 1.0, 0.0)

    def softmax_pv(scores, values, sink2):
        m = _row_max(scores[0])
        for s in scores[1:]:
            m = jnp.maximum(m, _row_max(s))
        if sink2 is not None:
            m = jnp.maximum(m, sink2)
        oa = dot(_exp2_bf16(scores[0], m), values[0])
        for s, v in zip(scores[1:], values[1:]):
            oa = oa + dot(_exp2_bf16(s, m), v)
        den = oa[:, LANES:2 * LANES]
        if sink2 is not None:
            den = den + jnp.exp2(sink2 - m)
        return oa[:, 0:LANES] / den

    def mixer_b(latent):
        if latent:
            start = pl.multiple_of(jnp.clip(n * ATT_ROWS - WINDOW, 0, SEQ - 2 * ATT_ROWS), LANES)
            qpos = n * ATT_ROWS + lax.broadcasted_iota(jnp.int32, (ATT_ROWS, 2 * ATT_ROWS), 0)
            kpos = start + lax.broadcasted_iota(jnp.int32, (ATT_ROWS, 2 * ATT_ROWS), 1)
            valid = jnp.abs(qpos - kpos) <= WINDOW
        for p in range(B_HEADS // 2):
            qt = qb_ref[:, LANES * p:LANES * p + LANES].astype(F32)
            outs = []
            for r in range(2):
                h = 2 * p + r
                j = h // (B_HEADS // B_KV_HEADS)
                var = 0 if j == r else 1
                krows = slice(LANES * var, LANES * var + LANES)
                vcols = slice(256 * var, 256 * var + 256)
                qm = (qt * head_mask(64 * r, 64)).astype(BF16)
                sink2 = sink_ref[h] * LOG2E
                s_ctx = dot(qm, kbc_ref[krows, :])
                if latent:
                    s_win = dot(qm, kbl_ref[krows, pl.ds(start, 2 * ATT_ROWS)])
                    s_win = jnp.where(valid, s_win, NEG_BIG)
                    outs.append(softmax_pv([s_win, s_ctx],
                                           [vbl_ref[pl.ds(start, 2 * ATT_ROWS), vcols],
                                            vbc_ref[:, vcols]], sink2))
                else:
                    outs.append(softmax_pv([s_ctx], [vbc_ref[:, vcols]], sink2))
            mix_ref[:, A_WIDTH + LANES * p:A_WIDTH + LANES * p + LANES] = (
                jnp.where(low_half, outs[0], outs[1]).astype(BF16))

    def mixer_c(latent):
        for t in range(C_HEADS // 2):
            qt = qc_ref[:, LANES * t:LANES * t + LANES].astype(F32)
            krows = slice(LANES * t, LANES * t + LANES)
            vcols = slice(256 * t, 256 * t + 256)
            outs = []
            for r in range(2):
                maps = []
                for mp in range(2):
                    qm = (qt * head_mask(64 * r + C_QK_DIM * mp, C_QK_DIM)).astype(BF16)
                    s_ctx = dot(qm, kcc_ref[krows, :])
                    if latent:
                        s_lat = dot(qm, kcl_ref[krows, :])
                        maps.append(softmax_pv([s_lat, s_ctx],
                                               [vcl_ref[:, vcols], vcc_ref[:, vcols]], None))
                    else:
                        maps.append(softmax_pv([s_ctx], [vcc_ref[:, vcols]], None))
                outs.append(maps[0] - lam * maps[1])
            o = jnp.where(low_half, outs[0], outs[1])
            o2 = o * o
            ms_lo = jnp.sum(jnp.where(low_half, o2, 0.0), axis=-1, keepdims=True) * (1.0 / C_V_DIM)
            ms_hi = jnp.sum(jnp.where(low_half, 0.0, o2), axis=-1, keepdims=True) * (1.0 / C_V_DIM)
            rs = jnp.where(low_half, lax.rsqrt(ms_lo + EPS), lax.rsqrt(ms_hi + EPS))
            off = A_WIDTH + 512 + LANES * t
            mix_ref[:, off:off + LANES] = (o * rs * subln_ref[...]).astype(BF16)

    @pl.when(n < n_lat_qblocks)
    def _():
        mixer_c(True)
        mixer_b(True)
        mixer_a()

    @pl.when(n >= n_lat_qblocks)
    def _():
        mixer_c(False)
        mixer_b(False)
        mixer_a()


def _mixers(u, vn, qb, kbt, vb, qc, kct, vc, sink, c_lam, wcat, abias, subln, lam_init, with_ctx):
    n_lat_qblocks = SEQ // ATT_ROWS
    n_qblocks = n_lat_qblocks + (1 if with_ctx else 0)
    lat_q_total = NLAT // ATT_ROWS
    out_rows = NTOK if with_ctx else NLAT

    def qrow(b, n):
        return (jnp.where(n < n_lat_qblocks, b * n_lat_qblocks + n, lat_q_total + b), 0)

    lat_rows = lambda b, n: (b, 0)
    ctx_rows = lambda b, n: (NLAT // CTX_LEN + b, 0)
    lat_cols = lambda b, n: (0, b)
    ctx_cols = lambda b, n: (0, NLAT // CTX_LEN + b)
    whole2 = lambda b, n: (0, 0)
    kernel = functools.partial(_mixer_kernel, lam_init=lam_init, n_lat_qblocks=n_lat_qblocks)
    return pl.pallas_call(
        kernel,
        grid=(BATCH, n_qblocks),
        in_specs=[
            pl.BlockSpec((ATT_ROWS, 256), qrow),
            pl.BlockSpec((ATT_ROWS, 256), qrow),
            pl.BlockSpec((ATT_ROWS, 512), qrow),
            pl.BlockSpec((ATT_ROWS, 256), qrow),
            pl.BlockSpec((256, SEQ), lat_cols),
            pl.BlockSpec((256, CTX_LEN), ctx_cols),
            pl.BlockSpec((SEQ, 512), lat_rows),
            pl.BlockSpec((CTX_LEN, 512), ctx_rows),
            pl.BlockSpec((256, SEQ), lat_cols),
            pl.BlockSpec((256, CTX_LEN), ctx_cols),
            pl.BlockSpec((SEQ, 512), lat_rows),
            pl.BlockSpec((CTX_LEN, 512), ctx_rows),
            pl.BlockSpec(memory_space=pltpu.SMEM),
            pl.BlockSpec((4, C_QK_DIM), whole2),
            pl.BlockSpec((2, CHUNK, 256), lambda b, n: (0, 0, 0)),
            pl.BlockSpec((CHUNK, 256), whole2),
            pl.BlockSpec((1, LANES), whole2),
        ],
        out_specs=pl.BlockSpec((ATT_ROWS, MIX_WIDTH), qrow),
        out_shape=jax.ShapeDtypeStruct((out_rows, MIX_WIDTH), BF16),
        compiler_params=_params(("arbitrary", "arbitrary"), 48),
        name="mixers",
    )(u, vn, qb, qc, kbt, kbt, vb, vb, kct, kct, vc, vc, sink, c_lam, wcat, abias, subln)


HALO = BF16_SUBLANES


SEAM_PAD = 8


def _ffn_kernel(mix_ref, mixp_ref, mixn_ref, x_ref, xp_ref, xn_ref,
                g1_ref, sh_ref, sc_ref, g2_ref, ng_ref,
                wout_ref, wup_ref, cw_ref, cb_ref, wd_ref,
                o_ref, mext_ref, xext_ref, lhs_ref, *scratch, n_lat_blocks, blocks_per_seq):
    *a_refs, y_ref = scratch
    top = slice(0, HALO)
    mid = slice(HALO, HALO + FFN_ROWS)
    bot = slice(HALO + FFN_ROWS, 2 * HALO + FFN_ROWS)
    i = pl.program_id(0)
    is_ctx = i >= n_lat_blocks
    seq_pos = i % blocks_per_seq
    no_prev = jnp.logical_or(is_ctx, seq_pos == 0)
    no_next = jnp.logical_or(is_ctx, seq_pos == blocks_per_seq - 1)

    mext_ref[top, :] = mixp_ref[...]
    mext_ref[mid, :] = mix_ref[...]
    mext_ref[bot, :] = mixn_ref[...]
    xext_ref[top, :] = xp_ref[...]
    xext_ref[mid, :] = x_ref[...]
    xext_ref[bot, :] = xn_ref[...]

    x1 = xext_ref[...] + g1_ref[0] * jnp.dot(mext_ref[...], wout_ref[...],
                                             preferred_element_type=F32)
    xext_ref[...] = x1
    lhs_ref[...] = _norm_modulate(x1, ng_ref[...], sh_ref[0], sc_ref[0]).astype(BF16)
    zeros = jnp.zeros((HALO, D_MODEL), BF16)
    lhs_ref[top, :] = jnp.where(no_prev, zeros, lhs_ref[top, :])
    lhs_ref[bot, :] = jnp.where(no_next, zeros, lhs_ref[bot, :])

    def conv(a, pad, cols, mprev=None, mnext=None):
        n = a.shape[0]
        rows = slice(pad, n - pad)
        prev = pltpu.roll(a, 1, 0)[rows, :]
        nxt = pltpu.roll(a, n - 1, 0)[rows, :]
        if mprev is not None:
            prev = prev * mprev
            nxt = nxt * mnext
        return (prev * cw_ref[0:1, cols] + a[rows, :] * cw_ref[1:2, cols] + nxt * cw_ref[2:3, cols]
                + cb_ref[:, cols])

    def activate(gate, val):
        return ((gate / (1.0 + jnp.exp(-gate))) * val).astype(BF16)

    seams = []
    for seam in range(CTX_LEN, FFN_ROWS, CTX_LEN):
        lo = seam - BF16_SUBLANES
        row = lo + lax.broadcasted_iota(jnp.int32, (2 * BF16_SUBLANES, FF_TILE), 0)
        mprev = jnp.where(jnp.logical_and(is_ctx, row == seam), 0.0, 1.0)
        mnext = jnp.where(jnp.logical_and(is_ctx, row == seam - 1), 0.0, 1.0)
        seams.append((lo, mprev, mnext))

    for t in range(N_FF_TILES):
        gcols = slice(FF_TILE * t, FF_TILE * t + FF_TILE)
        vcols = slice(D_FF + FF_TILE * t, D_FF + FF_TILE * t + FF_TILE)
        ag_ref, av_ref = a_refs[2 * (t % 2)], a_refs[2 * (t % 2) + 1]
        ag_ref[...] = jnp.dot(lhs_ref[...], wup_ref[:, gcols], preferred_element_type=F32)
        av_ref[...] = jnp.dot(lhs_ref[...], wup_ref[:, vcols], preferred_element_type=F32)
        y_ref[:, gcols] = activate(conv(ag_ref[...], HALO, gcols), conv(av_ref[...], HALO, vcols))
        for lo, mprev, mnext in seams:
            slab = slice(HALO + lo - SEAM_PAD, HALO + lo + 2 * BF16_SUBLANES + SEAM_PAD)
            y_ref[lo:lo + 2 * BF16_SUBLANES, gcols] = activate(
                conv(ag_ref[slab, :], SEAM_PAD, gcols, mprev, mnext),
                conv(av_ref[slab, :], SEAM_PAD, vcols, mprev, mnext))

    acc = jnp.dot(y_ref[...], wd_ref[...], preferred_element_type=F32)
    o_ref[...] = xext_ref[mid, :] + g2_ref[0] * acc


def _ffn(mix, xs, mod, norm_g, w_out, w_up, conv_w, conv_b, w_down):
    n_rows = mix.shape[0]
    n_lat_blocks = NLAT // FFN_ROWS
    blocks_per_seq = SEQ // FFN_ROWS
    halo_per_block = FFN_ROWS // HALO
    ext_rows = FFN_ROWS + 2 * HALO

    row = lambda i: (i, 0)
    whole = lambda i: (0, 0)
    resident = dict(pipeline_mode=pl.Buffered(1))

    def cond_row(i):
        return jnp.where(i < n_lat_blocks, i // blocks_per_seq, CTX_COND_ROW)

    modspec = lambda k: pl.BlockSpec((1, 1, D_MODEL), lambda i: (cond_row(i) * 6 + k, 0, 0))
    main = pl.BlockSpec((FFN_ROWS, D_MODEL), row)
    prev_halo = pl.BlockSpec((HALO, D_MODEL),
                             lambda i: (jnp.maximum(i * halo_per_block - 1, 0), 0))
    next_halo = pl.BlockSpec((HALO, D_MODEL),
                             lambda i: (jnp.minimum((i + 1) * halo_per_block,
                                                    n_rows // HALO - 1), 0))
    in_specs = [
        main, prev_halo, next_halo, main, prev_halo, next_halo,
        modspec(2), modspec(3), modspec(4), modspec(5), pl.BlockSpec((1, D_MODEL), whole),
        pl.BlockSpec((MIX_WIDTH, D_MODEL), whole, **resident),
        pl.BlockSpec((D_MODEL, 2 * D_FF), whole, **resident),
        pl.BlockSpec((3, 2 * D_FF), whole, **resident),
        pl.BlockSpec((1, 2 * D_FF), whole, **resident),
        pl.BlockSpec((D_FF, D_MODEL), whole, **resident),
    ]
    a_shape = pltpu.VMEM((ext_rows, FF_TILE), F32)
    return pl.pallas_call(
        functools.partial(_ffn_kernel, n_lat_blocks=n_lat_blocks, blocks_per_seq=blocks_per_seq),
        grid=(n_rows // FFN_ROWS,),
        in_specs=in_specs,
        out_specs=main,
        out_shape=jax.ShapeDtypeStruct((n_rows, D_MODEL), F32),
        scratch_shapes=[pltpu.VMEM((ext_rows, MIX_WIDTH), BF16),
                        pltpu.VMEM((ext_rows, D_MODEL), F32),
                        pltpu.VMEM((ext_rows, D_MODEL), BF16),
                        a_shape, a_shape, a_shape, a_shape,
                        pltpu.VMEM((FFN_ROWS, D_FF), BF16)],
        compiler_params=_params(("arbitrary",), 56),
        name="ffn",
    )(mix, mix, mix, xs, xs, xs, mod, mod, mod, mod, norm_g, w_out, w_up, conv_w, conv_b, w_down)


def _rope_tables(head_dim):
    n_freq = head_dim // 4
    rows = SEQ // GRID_W
    row = jnp.repeat(jnp.arange(rows, dtype=F32), GRID_W)
    col = jnp.tile(jnp.arange(GRID_W, dtype=F32), rows)
    inv_freq = ROPE_BASE ** (-jnp.arange(n_freq, dtype=F32) / n_freq)
    ang = jnp.stack([row[:, None] * inv_freq, col[:, None] * inv_freq], axis=1)
    lane = np.arange(LANES)
    within = lane % head_dim
    axis = within // (2 * n_freq)
    second = (within % (2 * n_freq)) >= n_freq
    ang = ang[:, axis, lane % n_freq]
    cos = jnp.cos(ang)
    sin = jnp.where(jnp.asarray(second)[None, :], jnp.sin(ang), -jnp.sin(ang))
    cos = jnp.concatenate([cos, jnp.ones((PROJ_ROWS, LANES), F32)], axis=0)
    sin = jnp.concatenate([sin, jnp.zeros((PROJ_ROWS, LANES), F32)], axis=0)
    return cos, sin


def _group_matrix(width):
    idx = np.arange(256)
    g = (idx[:, None] // width == idx[None, :] // width).astype(np.float32) / width
    return jnp.asarray(g, dtype=BF16)


def _constants():
    cosb, sinb = _rope_tables(B_HEAD_DIM)
    cosc, sinc = _rope_tables(C_QK_DIM)
    return dict(cosb=cosb, sinb=sinb, cosc=cosc, sinc=sinc,
                g64=_group_matrix(64), g32=_group_matrix(32))


def kernel(x, c, ctx, c_ctx, w_ada, b_ada, norm1_g, norm2_g, w_in, a_ws, a_bs, b_qnorm, b_knorm,
           b_sink, c_qnorm, c_knorm, c_lam, c_subln, w_out, w_up, conv_w, conv_b, w_down):
    consts = _constants()
    xs = jnp.concatenate([x.reshape(NLAT, D_MODEL), ctx.reshape(NCTX, D_MODEL)], axis=0)

    cond = jnp.zeros((COND_ROWS, D_MODEL), F32).at[:BATCH].set(c).at[CTX_COND_ROW].set(c_ctx)
    mod_all = _modulation_all(cond, w_ada, b_ada)
    mod_all = mod_all.reshape(DEPTH, COND_ROWS * 6, 1, D_MODEL)

    for i in range(DEPTH):
        last = i == DEPTH - 1
        lam_init = 0.8 - 0.6 * math.exp(-0.3 * i)
        mod = mod_all[i]
        zeros = jnp.zeros((4, LANES), F32)
        gains = jnp.concatenate([
            jnp.tile(b_qnorm[i] * (B_HEAD_DIM ** -0.5 * LOG2E), LANES // B_HEAD_DIM)[None],
            jnp.tile(b_knorm[i], LANES // B_HEAD_DIM)[None],
            jnp.tile(c_qnorm[i] * (C_QK_DIM ** -0.5 * LOG2E), LANES // C_QK_DIM)[None],
            jnp.tile(c_knorm[i], LANES // C_QK_DIM)[None],
            zeros], axis=0)
        u, vn, qb, kbt, vb, qc, kct, vc = _proj(
            xs, mod, norm1_g[i][None], w_in[i].astype(BF16), consts, gains)

        wcat = jnp.concatenate([a_ws[i][0::2], a_ws[i][1::2]], axis=-1).astype(BF16)
        abias = jnp.repeat(a_bs[i].T, A_WIDTH // A_GROUPS, axis=1)
        subln = jnp.tile(c_subln[i] * (1.0 - lam_init), LANES // C_V_DIM)[None]
        mix = _mixers(u, vn, qb, kbt, vb, qc, kct, vc, b_sink[i], c_lam[i], wcat, abias, subln,
                      lam_init, with_ctx=not last)

        xs = _ffn(mix, xs, mod, norm2_g[i][None], w_out[i].astype(BF16), w_up[i].astype(BF16),
                  conv_w[i].reshape(3, 2 * D_FF), conv_b[i][None], w_down[i].astype(BF16))

    return xs.reshape(BATCH, SEQ, D_MODEL)
```

```python
import functools
import math

import jax
import jax.numpy as jnp
import numpy as np
from jax import lax
from jax.experimental import pallas as pl
from jax.experimental.pallas import tpu as pltpu

F32 = jnp.float32
BF16 = jnp.bfloat16

D_MODEL = 1024
BATCH = 16
SEQ = 2048
DEPTH = 4
GRID_W = 64
CTX_LEN = 256
A_WIDTH = 256
A_GROUPS = 4
CHUNK = 128
B_HEAD_DIM = 64
B_HEADS = 8
B_KV_HEADS = 2
WINDOW = 128
C_V_DIM = 64
C_QK_DIM = 32
C_HEADS = 4
PROJ_WIDTH = 2048
MIX_WIDTH = 1024
D_FF = 2816
ROPE_BASE = 10000.0
EPS = 1e-6

LOG2E = 1.4426950408889634
NEG_BIG = -1e30

LANES = 128
BF16_SUBLANES = 16

NLAT = BATCH * SEQ
NCTX = BATCH * CTX_LEN
NTOK = NLAT + NCTX
COND_ROWS = 24
CTX_COND_ROW = BATCH

PROJ_ROWS = 1024
ATT_ROWS = 256
ATT_BLOCK = 512
FFN_ROWS = 512
FF_TILE = 256
N_FF_TILES = D_FF // FF_TILE
ADA_TILE = 1536

OFF_AU, OFF_AV, OFF_BQ, OFF_BK, OFF_BV, OFF_CQ, OFF_CK, OFF_CV = (
    0, 256, 512, 1024, 1152, 1280, 1536, 1792)


def _params(sem, vmem_mb):
    return pltpu.CompilerParams(dimension_semantics=sem,
                                vmem_limit_bytes=vmem_mb * 1024 * 1024)


def _mod_kernel(cond_ref, w_ref, b_ref, o_ref):
    c = cond_ref[...]
    s = c / (1.0 + jnp.exp(-c))
    o_ref[0] = jnp.dot(s.astype(BF16), w_ref[0].astype(BF16),
                       preferred_element_type=F32) + b_ref[0]


def _modulation_all(cond, w_ada, b_ada):
    n_tiles = (6 * D_MODEL) // ADA_TILE
    return pl.pallas_call(
        _mod_kernel,
        grid=(DEPTH, n_tiles),
        in_specs=[
            pl.BlockSpec((COND_ROWS, D_MODEL), lambda l, j: (0, 0)),
            pl.BlockSpec((1, D_MODEL, ADA_TILE), lambda l, j: (l, 0, j)),
            pl.BlockSpec((1, 1, ADA_TILE), lambda l, j: (l, 0, j)),
        ],
        out_specs=pl.BlockSpec((1, COND_ROWS, ADA_TILE), lambda l, j: (l, 0, j)),
        out_shape=jax.ShapeDtypeStruct((DEPTH, COND_ROWS, 6 * D_MODEL), F32),
        compiler_params=_params(("arbitrary", "arbitrary"), 32),
        name="modulation",
    )(cond, w_ada, b_ada.reshape(DEPTH, 1, 6 * D_MODEL))


def _norm_modulate(x, gain, shift, scale):
    ms = jnp.mean(x * x, axis=-1, keepdims=True)
    return (x * lax.rsqrt(ms + EPS) * gain) * (1.0 + scale) + shift


def _gelu_tanh(x):
    return 0.5 * x * (1.0 + jnp.tanh(math.sqrt(2.0 / math.pi) * (x + 0.044715 * (x * x * x))))


def _group_mean_sq(t, g):
    return jnp.dot((t * t).astype(BF16), g, preferred_element_type=F32)


def _swap_halves(t, half):
    lane = lax.broadcasted_iota(jnp.int32, t.shape, 1)
    first = (lane & (2 * half - 1)) < half
    return jnp.where(first, pltpu.roll(t, LANES - half, 1), pltpu.roll(t, half, 1))


def _lane_iota(shape):
    return lax.broadcasted_iota(jnp.int32, shape, 1)


def _proj_kernel(x_ref, sh_ref, sc_ref, ng_ref, w_ref, g64_ref, g32_ref,
                 cosb_ref, sinb_ref, cosc_ref, sinc_ref, gains_ref,
                 u_ref, vn_ref, qb_ref, kbt_ref, vb_ref, qc_ref, kct_ref, vc_ref):
    h = _norm_modulate(x_ref[...], ng_ref[...], sh_ref[0], sc_ref[0])
    z = jnp.dot(h.astype(BF16), w_ref[...], preferred_element_type=F32)

    u_ref[...] = _gelu_tanh(z[:, OFF_AU:OFF_AU + A_WIDTH])
    v = _gelu_tanh(z[:, OFF_AV:OFF_AV + A_WIDTH])
    vn_ref[...] = (v * lax.rsqrt(_group_mean_sq(v, g64_ref[...]) + EPS)).astype(BF16)

    def norm_rope(t, ms, gain, cos_ref, sin_ref, half):
        tg = t * gain
        return lax.rsqrt(ms + EPS) * (tg * cos_ref[...] + _swap_halves(tg, half) * sin_ref[...])

    for t in range(4):
        lo = OFF_BQ + 256 * (t // 2)
        if t % 2 == 0:
            q2 = z[:, lo:lo + 256]
            ms2 = _group_mean_sq(q2, g64_ref[...])
        s = slice(LANES * (t % 2), LANES * (t % 2) + LANES)
        qb_ref[:, LANES * t:LANES * t + LANES] = norm_rope(
            q2[:, s], ms2[:, s], gains_ref[0:1, :], cosb_ref, sinb_ref, 16).astype(BF16)

    kb = z[:, OFF_BK:OFF_BK + LANES]
    kb = norm_rope(kb, _group_mean_sq(kb, g64_ref[0:LANES, 0:LANES]),
                   gains_ref[1:2, :], cosb_ref, sinb_ref, 16)
    kbt = kb.T.astype(BF16)
    kbt_ref[0:LANES, :] = kbt
    kbt_ref[LANES:LANES + 64, :] = kbt[64:LANES, :]
    kbt_ref[LANES + 64:2 * LANES, :] = kbt[0:64, :]

    vb = z[:, OFF_BV:OFF_BV + LANES]
    ones = jnp.ones((PROJ_ROWS, LANES), BF16)
    vb_ref[:, 0:LANES] = vb.astype(BF16)
    vb_ref[:, LANES:2 * LANES] = ones
    vb_ref[:, 2 * LANES:3 * LANES] = pltpu.roll(vb, 64, 1).astype(BF16)
    vb_ref[:, 3 * LANES:4 * LANES] = ones

    cq = z[:, OFF_CQ:OFF_CQ + 256]
    msq = _group_mean_sq(cq, g32_ref[...])
    ck = z[:, OFF_CK:OFF_CK + 256]
    msk = _group_mean_sq(ck, g32_ref[...])
    for t in range(2):
        s = slice(LANES * t, LANES * t + LANES)
        qc_ref[:, s] = norm_rope(cq[:, s], msq[:, s], gains_ref[2:3, :],
                                 cosc_ref, sinc_ref, 8).astype(BF16)
        kc = norm_rope(ck[:, s], msk[:, s], gains_ref[3:4, :], cosc_ref, sinc_ref, 8)
        kct_ref[s, :] = kc.T.astype(BF16)

    cv = z[:, OFF_CV:OFF_CV + 256]
    vc_ref[:, 0:LANES] = cv[:, 0:LANES].astype(BF16)
    vc_ref[:, LANES:2 * LANES] = ones
    vc_ref[:, 2 * LANES:3 * LANES] = cv[:, LANES:2 * LANES].astype(BF16)
    vc_ref[:, 3 * LANES:4 * LANES] = ones


def _proj(xs, mod, norm_g, w_in, consts, gains):
    n_lat_blocks = NLAT // PROJ_ROWS
    blocks_per_seq = SEQ // PROJ_ROWS

    def cond_row(i):
        return jnp.where(i < n_lat_blocks, i // blocks_per_seq, CTX_COND_ROW)

    def pos_block(i):
        return jnp.where(i < n_lat_blocks, i % blocks_per_seq, blocks_per_seq)

    row = lambda i: (i, 0)
    col = lambda i: (0, i)
    whole = lambda i: (0, 0)
    pos = lambda i: (pos_block(i), 0)
    out_shape = (
        jax.ShapeDtypeStruct((NTOK, 256), F32),
        jax.ShapeDtypeStruct((NTOK, 256), BF16),
        jax.ShapeDtypeStruct((NTOK, 512), BF16),
        jax.ShapeDtypeStruct((256, NTOK), BF16),
        jax.ShapeDtypeStruct((NTOK, 512), BF16),
        jax.ShapeDtypeStruct((NTOK, 256), BF16),
        jax.ShapeDtypeStruct((256, NTOK), BF16),
        jax.ShapeDtypeStruct((NTOK, 512), BF16),
    )
    out_specs = (
        pl.BlockSpec((PROJ_ROWS, 256), row),
        pl.BlockSpec((PROJ_ROWS, 256), row),
        pl.BlockSpec((PROJ_ROWS, 512), row),
        pl.BlockSpec((256, PROJ_ROWS), col),
        pl.BlockSpec((PROJ_ROWS, 512), row),
        pl.BlockSpec((PROJ_ROWS, 256), row),
        pl.BlockSpec((256, PROJ_ROWS), col),
        pl.BlockSpec((PROJ_ROWS, 512), row),
    )
    return pl.pallas_call(
        _proj_kernel,
        grid=(NTOK // PROJ_ROWS,),
        in_specs=[
            pl.BlockSpec((PROJ_ROWS, D_MODEL), row),
            pl.BlockSpec((1, 1, D_MODEL), lambda i: (cond_row(i) * 6 + 0, 0, 0)),
            pl.BlockSpec((1, 1, D_MODEL), lambda i: (cond_row(i) * 6 + 1, 0, 0)),
            pl.BlockSpec((1, D_MODEL), whole),
            pl.BlockSpec((D_MODEL, PROJ_WIDTH), whole),
            pl.BlockSpec((256, 256), whole),
            pl.BlockSpec((256, 256), whole),
            pl.BlockSpec((PROJ_ROWS, LANES), pos),
            pl.BlockSpec((PROJ_ROWS, LANES), pos),
            pl.BlockSpec((PROJ_ROWS, LANES), pos),
            pl.BlockSpec((PROJ_ROWS, LANES), pos),
            pl.BlockSpec((8, LANES), whole),
        ],
        out_specs=out_specs,
        out_shape=out_shape,
        compiler_params=_params(("arbitrary",), 48),
        name="proj",
    )(xs, mod, mod, norm_g, w_in, consts["g64"], consts["g32"],
      consts["cosb"], consts["sinb"], consts["cosc"], consts["sinc"], gains)


def _exp2_bf16(s, m):
    return jnp.exp2(s - m).astype(BF16)


def _row_max(s):
    return jnp.max(s, axis=-1, keepdims=True)


def _mixer_kernel(u_ref, vn_ref, qb_ref, qc_ref,
                  kbl_ref, kbc_ref, vbl_ref, vbc_ref,
                  kcl_ref, kcc_ref, vcl_ref, vcc_ref,
                  sink_ref, lam_ref, wcat_ref, abias_ref, subln_ref,
                  mix_ref, *, lam_init, steps_per_batch):
    s = pl.program_id(1)
    lane = _lane_iota((ATT_ROWS, LANES))
    low_half = lane < 64
    dot = functools.partial(jnp.dot, preferred_element_type=F32)

    def mixer_a(rb):
        for c in range(ATT_ROWS // CHUNK):
            rows = pl.ds(rb + CHUNK * c, CHUNK)
            for t in range(2):
                cols = slice(LANES * t, LANES * t + LANES)
                vt = vn_ref[rows, cols].astype(F32)
                keep_lo = _lane_iota(vt.shape) < 64
                rhs = jnp.concatenate([jnp.where(keep_lo, vt, 0.0), jnp.where(keep_lo, 0.0, vt)],
                                      axis=0).astype(BF16)
                mixed = dot(wcat_ref[t], rhs) + abias_ref[:, cols]
                mix_ref[rows, cols] = (u_ref[rows, cols] * mixed).astype(BF16)

    lp = lam_ref[...]
    lam = (jnp.exp(jnp.sum(lp[0:1] * lp[1:2], axis=-1, keepdims=True))
           - jnp.exp(jnp.sum(lp[2:3] * lp[3:4], axis=-1, keepdims=True)) + lam_init)

    def head_mask(offset, width):
        return jnp.where((lane >= offset) & (lane < offset + width), 1.0, 0.0)

    def softmax_pv(scores, values, sink2):
        m = _row_max(scores[0])
        for s in scores[1:]:
            m = jnp.maximum(m, _row_max(s))
        if sink2 is not None:
            m = jnp.maximum(m, sink2)
        oa = dot(_exp2_bf16(scores[0], m), values[0])
        for s, v in zip(scores[1:], values[1:]):
            oa = oa + dot(_exp2_bf16(s, m), v)
        den = oa[:, LANES:2 * LANES]
        if sink2 is not None:
            den = den + jnp.exp2(sink2 - m)
        return oa[:, 0:LANES] / den

    def mixer_b(latent, rb, q0):
        qrows = pl.ds(rb, ATT_ROWS)
        if latent:
            start = pl.multiple_of(jnp.clip(q0 - WINDOW, 0, SEQ - 2 * ATT_ROWS), LANES)
            qpos = q0 + lax.broadcasted_iota(jnp.int32, (ATT_ROWS, 2 * ATT_ROWS), 0)
            kpos = start + lax.broadcasted_iota(jnp.int32, (ATT_ROWS, 2 * ATT_ROWS), 1)
            valid = jnp.abs(qpos - kpos) <= WINDOW
        for p in range(B_HEADS // 2):
            qt = qb_ref[qrows, LANES * p:LANES * p + LANES].astype(F32)
            outs = []
            for r in range(2):
                h = 2 * p + r
                j = h // (B_HEADS // B_KV_HEADS)
                var = 0 if j == r else 1
                krows = slice(LANES * var, LANES * var + LANES)
                vcols = slice(256 * var, 256 * var + 256)
                qm = (qt * head_mask(64 * r, 64)).astype(BF16)
                sink2 = sink_ref[h] * LOG2E
                s_ctx = dot(qm, kbc_ref[krows, :])
                if latent:
                    s_win = dot(qm, kbl_ref[krows, pl.ds(start, 2 * ATT_ROWS)])
                    s_win = jnp.where(valid, s_win, NEG_BIG)
                    outs.append(softmax_pv([s_win, s_ctx],
                                           [vbl_ref[pl.ds(start, 2 * ATT_ROWS), vcols],
                                            vbc_ref[:, vcols]], sink2))
                else:
                    outs.append(softmax_pv([s_ctx], [vbc_ref[:, vcols]], sink2))
            mix_ref[qrows, A_WIDTH + LANES * p:A_WIDTH + LANES * p + LANES] = (
                jnp.where(low_half, outs[0], outs[1]).astype(BF16))

    def mixer_c(latent, rb):
        qrows = pl.ds(rb, ATT_ROWS)
        for t in range(C_HEADS // 2):
            qt = qc_ref[qrows, LANES * t:LANES * t + LANES].astype(F32)
            krows = slice(LANES * t, LANES * t + LANES)
            vcols = slice(256 * t, 256 * t + 256)
            outs = []
            for r in range(2):
                maps = []
                for mp in range(2):
                    qm = (qt * head_mask(64 * r + C_QK_DIM * mp, C_QK_DIM)).astype(BF16)
                    s_ctx = dot(qm, kcc_ref[krows, :])
                    if latent:
                        s_lat = dot(qm, kcl_ref[krows, :])
                        maps.append(softmax_pv([s_lat, s_ctx],
                                               [vcl_ref[:, vcols], vcc_ref[:, vcols]], None))
                    else:
                        maps.append(softmax_pv([s_ctx], [vcc_ref[:, vcols]], None))
                outs.append(maps[0] - lam * maps[1])
            o = jnp.where(low_half, outs[0], outs[1])
            o2 = o * o
            ms_lo = jnp.sum(jnp.where(low_half, o2, 0.0), axis=-1, keepdims=True) * (1.0 / C_V_DIM)
            ms_hi = jnp.sum(jnp.where(low_half, 0.0, o2), axis=-1, keepdims=True) * (1.0 / C_V_DIM)
            rs = jnp.where(low_half, lax.rsqrt(ms_lo + EPS), lax.rsqrt(ms_hi + EPS))
            off = A_WIDTH + 512 + LANES * t
            mix_ref[qrows, off:off + LANES] = (o * rs * subln_ref[...]).astype(BF16)

    @pl.when(s < 2 * steps_per_batch)
    def _():
        q_base = (s % steps_per_batch) * ATT_BLOCK
        subs = [(ATT_ROWS * k, q_base + ATT_ROWS * k) for k in range(ATT_BLOCK // ATT_ROWS)]
        for rb, _ in subs:
            mixer_c(True, rb)
        for rb, q0 in subs:
            mixer_b(True, rb, q0)
        for rb, _ in subs:
            mixer_a(rb)

    @pl.when(s >= 2 * steps_per_batch)
    def _():
        rb = pl.multiple_of((s - 2 * steps_per_batch) * CTX_LEN, CTX_LEN)
        mixer_c(False, rb)
        mixer_b(False, rb, None)
        mixer_a(rb)


def _mixers(u, vn, qb, kbt, vb, qc, kct, vc, sink, c_lam, wcat, abias, subln, lam_init, with_ctx):
    steps_per_batch = SEQ // ATT_BLOCK
    n_steps = 2 * steps_per_batch + (2 if with_ctx else 0)
    out_rows = NTOK if with_ctx else NLAT

    def second(s):
        is_ctx = s >= 2 * steps_per_batch
        return jnp.where(is_ctx, s - 2 * steps_per_batch, s // steps_per_batch)

    def qrow(p, s):
        latent = (2 * p + s // steps_per_batch) * steps_per_batch + s % steps_per_batch
        return (jnp.where(s < 2 * steps_per_batch, latent, NLAT // ATT_BLOCK + p), 0)

    lat_batch = lambda p, s: 2 * p + jnp.minimum(s // steps_per_batch, 1)
    lat_rows = lambda p, s: (lat_batch(p, s), 0)
    lat_cols = lambda p, s: (0, lat_batch(p, s))
    ctx_rows = lambda p, s: (NLAT // CTX_LEN + 2 * p + second(s), 0)
    ctx_cols = lambda p, s: (0, NLAT // CTX_LEN + 2 * p + second(s))
    whole2 = lambda p, s: (0, 0)
    kernel = functools.partial(_mixer_kernel, lam_init=lam_init, steps_per_batch=steps_per_batch)
    return pl.pallas_call(
        kernel,
        grid=(BATCH // 2, n_steps),
        in_specs=[
            pl.BlockSpec((ATT_BLOCK, 256), qrow),
            pl.BlockSpec((ATT_BLOCK, 256), qrow),
            pl.BlockSpec((ATT_BLOCK, 512), qrow),
            pl.BlockSpec((ATT_BLOCK, 256), qrow),
            pl.BlockSpec((256, SEQ), lat_cols),
            pl.BlockSpec((256, CTX_LEN), ctx_cols),
            pl.BlockSpec((SEQ, 512), lat_rows),
            pl.BlockSpec((CTX_LEN, 512), ctx_rows),
            pl.BlockSpec((256, SEQ), lat_cols),
            pl.BlockSpec((256, CTX_LEN), ctx_cols),
            pl.BlockSpec((SEQ, 512), lat_rows),
            pl.BlockSpec((CTX_LEN, 512), ctx_rows),
            pl.BlockSpec(memory_space=pltpu.SMEM),
            pl.BlockSpec((4, C_QK_DIM), whole2),
            pl.BlockSpec((2, CHUNK, 256), lambda p, s: (0, 0, 0)),
            pl.BlockSpec((CHUNK, 256), whole2),
            pl.BlockSpec((1, LANES), whole2),
        ],
        out_specs=pl.BlockSpec((ATT_BLOCK, MIX_WIDTH), qrow),
        out_shape=jax.ShapeDtypeStruct((out_rows, MIX_WIDTH), BF16),
        compiler_params=_params(("arbitrary", "arbitrary"), 48),
        name="mixers",
    )(u, vn, qb, qc, kbt, kbt, vb, vb, kct, kct, vc, vc, sink, c_lam, wcat, abias, subln)


HALO = BF16_SUBLANES


SEAM_PAD = 8
PROLOGUE_PIECES = 2


def _ffn_kernel(mix_ref, mixp_ref, mixn_ref, x_ref, xp_ref, xn_ref,
                g1_ref, sh_ref, sc_ref, g2_ref, ng_ref,
                wout_ref, wup_ref, cw_ref, cb_ref, wd_ref,
                o_ref, mext_ref, xext_ref, lhs_ref, *scratch, n_lat_blocks, blocks_per_seq):
    *a_refs, y_ref = scratch
    top = slice(0, HALO)
    mid = slice(HALO, HALO + FFN_ROWS)
    bot = slice(HALO + FFN_ROWS, 2 * HALO + FFN_ROWS)
    i = pl.program_id(0)
    is_ctx = i >= n_lat_blocks
    seq_pos = i % blocks_per_seq
    no_prev = jnp.logical_or(is_ctx, seq_pos == 0)
    no_next = jnp.logical_or(is_ctx, seq_pos == blocks_per_seq - 1)

    mext_ref[top, :] = mixp_ref[...]
    mext_ref[mid, :] = mix_ref[...]
    mext_ref[bot, :] = mixn_ref[...]
    xext_ref[top, :] = xp_ref[...]
    xext_ref[mid, :] = x_ref[...]
    xext_ref[bot, :] = xn_ref[...]

    piece = (FFN_ROWS + 2 * HALO) // PROLOGUE_PIECES
    for p in range(PROLOGUE_PIECES):
        rows = slice(piece * p, piece * p + piece)
        x1 = xext_ref[rows, :] + g1_ref[0] * jnp.dot(mext_ref[rows, :], wout_ref[...],
                                                     preferred_element_type=F32)
        xext_ref[rows, :] = x1
        lhs_ref[rows, :] = _norm_modulate(x1, ng_ref[...], sh_ref[0], sc_ref[0]).astype(BF16)
    zeros = jnp.zeros((HALO, D_MODEL), BF16)
    lhs_ref[top, :] = jnp.where(no_prev, zeros, lhs_ref[top, :])
    lhs_ref[bot, :] = jnp.where(no_next, zeros, lhs_ref[bot, :])

    def conv(a, pad, cols, mprev=None, mnext=None):
        n = a.shape[0]
        rows = slice(pad, n - pad)
        prev = pltpu.roll(a, 1, 0)[rows, :]
        nxt = pltpu.roll(a, n - 1, 0)[rows, :]
        if mprev is not None:
            prev = prev * mprev
            nxt = nxt * mnext
        return (prev * cw_ref[0:1, cols] + a[rows, :] * cw_ref[1:2, cols] + nxt * cw_ref[2:3, cols]
                + cb_ref[:, cols])

    def activate(gate, val):
        return ((gate / (1.0 + jnp.exp(-gate))) * val).astype(BF16)

    seams = []
    for seam in range(CTX_LEN, FFN_ROWS, CTX_LEN):
        lo = seam - BF16_SUBLANES
        row = lo + lax.broadcasted_iota(jnp.int32, (2 * BF16_SUBLANES, FF_TILE), 0)
        mprev = jnp.where(jnp.logical_and(is_ctx, row == seam), 0.0, 1.0)
        mnext = jnp.where(jnp.logical_and(is_ctx, row == seam - 1), 0.0, 1.0)
        seams.append((lo, mprev, mnext))

    for t in range(N_FF_TILES):
        gcols = slice(FF_TILE * t, FF_TILE * t + FF_TILE)
        vcols = slice(D_FF + FF_TILE * t, D_FF + FF_TILE * t + FF_TILE)
        ag_ref, av_ref = a_refs[2 * (t % 2)], a_refs[2 * (t % 2) + 1]
        ag_ref[...] = jnp.dot(lhs_ref[...], wup_ref[:, gcols], preferred_element_type=F32)
        av_ref[...] = jnp.dot(lhs_ref[...], wup_ref[:, vcols], preferred_element_type=F32)
        y_ref[:, gcols] = activate(conv(ag_ref[...], HALO, gcols), conv(av_ref[...], HALO, vcols))
        for lo, mprev, mnext in seams:
            slab = slice(HALO + lo - SEAM_PAD, HALO + lo + 2 * BF16_SUBLANES + SEAM_PAD)
            y_ref[lo:lo + 2 * BF16_SUBLANES, gcols] = activate(
                conv(ag_ref[slab, :], SEAM_PAD, gcols, mprev, mnext),
                conv(av_ref[slab, :], SEAM_PAD, vcols, mprev, mnext))

    acc = jnp.dot(y_ref[...], wd_ref[...], preferred_element_type=F32)
    o_ref[...] = xext_ref[mid, :] + g2_ref[0] * acc


def _ffn(mix, xs, mod, norm_g, w_out, w_up, conv_w, conv_b, w_down):
    n_rows = mix.shape[0]
    n_lat_blocks = NLAT // FFN_ROWS
    blocks_per_seq = SEQ // FFN_ROWS
    halo_per_block = FFN_ROWS // HALO
    ext_rows = FFN_ROWS + 2 * HALO

    row = lambda i: (i, 0)
    whole = lambda i: (0, 0)
    resident = dict(pipeline_mode=pl.Buffered(1))

    def cond_row(i):
        return jnp.where(i < n_lat_blocks, i // blocks_per_seq, CTX_COND_ROW)

    modspec = lambda k: pl.BlockSpec((1, 1, D_MODEL), lambda i: (cond_row(i) * 6 + k, 0, 0))
    main = pl.BlockSpec((FFN_ROWS, D_MODEL), row)
    prev_halo = pl.BlockSpec((HALO, D_MODEL),
                             lambda i: (jnp.maximum(i * halo_per_block - 1, 0), 0))
    next_halo = pl.BlockSpec((HALO, D_MODEL),
                             lambda i: (jnp.minimum((i + 1) * halo_per_block,
                                                    n_rows // HALO - 1), 0))
    in_specs = [
        main, prev_halo, next_halo, main, prev_halo, next_halo,
        modspec(2), modspec(3), modspec(4), modspec(5), pl.BlockSpec((1, D_MODEL), whole),
        pl.BlockSpec((MIX_WIDTH, D_MODEL), whole, **resident),
        pl.BlockSpec((D_MODEL, 2 * D_FF), whole, **resident),
        pl.BlockSpec((3, 2 * D_FF), whole, **resident),
        pl.BlockSpec((1, 2 * D_FF), whole, **resident),
        pl.BlockSpec((D_FF, D_MODEL), whole, **resident),
    ]
    a_shape = pltpu.VMEM((ext_rows, FF_TILE), F32)
    return pl.pallas_call(
        functools.partial(_ffn_kernel, n_lat_blocks=n_lat_blocks, blocks_per_seq=blocks_per_seq),
        grid=(n_rows // FFN_ROWS,),
        in_specs=in_specs,
        out_specs=main,
        out_shape=jax.ShapeDtypeStruct((n_rows, D_MODEL), F32),
        scratch_shapes=[pltpu.VMEM((ext_rows, MIX_WIDTH), BF16),
                        pltpu.VMEM((ext_rows, D_MODEL), F32),
                        pltpu.VMEM((ext_rows, D_MODEL), BF16),
                        a_shape, a_shape, a_shape, a_shape,
                        pltpu.VMEM((FFN_ROWS, D_FF), BF16)],
        compiler_params=_params(("arbitrary",), 56),
        name="ffn",
    )(mix, mix, mix, xs, xs, xs, mod, mod, mod, mod, norm_g, w_out, w_up, conv_w, conv_b, w_down)


def _rope_tables(head_dim):
    n_freq = head_dim // 4
    rows = SEQ // GRID_W
    row = jnp.repeat(jnp.arange(rows, dtype=F32), GRID_W)
    col = jnp.tile(jnp.arange(GRID_W, dtype=F32), rows)
    inv_freq = ROPE_BASE ** (-jnp.arange(n_freq, dtype=F32) / n_freq)
    ang = jnp.stack([row[:, None] * inv_freq, col[:, None] * inv_freq], axis=1)
    lane = np.arange(LANES)
    within = lane % head_dim
    axis = within // (2 * n_freq)
    second = (within % (2 * n_freq)) >= n_freq
    ang = ang[:, axis, lane % n_freq]
    cos = jnp.cos(ang)
    sin = jnp.where(jnp.asarray(second)[None, :], jnp.sin(ang), -jnp.sin(ang))
    cos = jnp.concatenate([cos, jnp.ones((PROJ_ROWS, LANES), F32)], axis=0)
    sin = jnp.concatenate([sin, jnp.zeros((PROJ_ROWS, LANES), F32)], axis=0)
    return cos, sin


def _group_matrix(width):
    idx = np.arange(256)
    g = (idx[:, None] // width == idx[None, :] // width).astype(np.float32) / width
    return jnp.asarray(g, dtype=BF16)


def _constants():
    cosb, sinb = _rope_tables(B_HEAD_DIM)
    cosc, sinc = _rope_tables(C_QK_DIM)
    return dict(cosb=cosb, sinb=sinb, cosc=cosc, sinc=sinc,
                g64=_group_matrix(64), g32=_group_matrix(32))


def kernel(x, c, ctx, c_ctx, w_ada, b_ada, norm1_g, norm2_g, w_in, a_ws, a_bs, b_qnorm, b_knorm,
           b_sink, c_qnorm, c_knorm, c_lam, c_subln, w_out, w_up, conv_w, conv_b, w_down):
    consts = _constants()
    xs = jnp.concatenate([x.reshape(NLAT, D_MODEL), ctx.reshape(NCTX, D_MODEL)], axis=0)

    cond = jnp.zeros((COND_ROWS, D_MODEL), F32).at[:BATCH].set(c).at[CTX_COND_ROW].set(c_ctx)
    mod_all = _modulation_all(cond, w_ada, b_ada)
    mod_all = mod_all.reshape(DEPTH, COND_ROWS * 6, 1, D_MODEL)

    for i in range(DEPTH):
        last = i == DEPTH - 1
        lam_init = 0.8 - 0.6 * math.exp(-0.3 * i)
        mod = mod_all[i]
        zeros = jnp.zeros((4, LANES), F32)
        gains = jnp.concatenate([
            jnp.tile(b_qnorm[i] * (B_HEAD_DIM ** -0.5 * LOG2E), LANES // B_HEAD_DIM)[None],
            jnp.tile(b_knorm[i], LANES // B_HEAD_DIM)[None],
            jnp.tile(c_qnorm[i] * (C_QK_DIM ** -0.5 * LOG2E), LANES // C_QK_DIM)[None],
            jnp.tile(c_knorm[i], LANES // C_QK_DIM)[None],
            zeros], axis=0)
        u, vn, qb, kbt, vb, qc, kct, vc = _proj(
            xs, mod, norm1_g[i][None], w_in[i].astype(BF16), consts, gains)

        wcat = jnp.concatenate([a_ws[i][0::2], a_ws[i][1::2]], axis=-1).astype(BF16)
        abias = jnp.repeat(a_bs[i].T, A_WIDTH // A_GROUPS, axis=1)
        subln = jnp.tile(c_subln[i] * (1.0 - lam_init), LANES // C_V_DIM)[None]
        mix = _mixers(u, vn, qb, kbt, vb, qc, kct, vc, b_sink[i], c_lam[i], wcat, abias, subln,
                      lam_init, with_ctx=not last)

        xs = _ffn(mix, xs, mod, norm2_g[i][None], w_out[i].astype(BF16), w_up[i].astype(BF16),
                  conv_w[i].reshape(3, 2 * D_FF), conv_b[i][None], w_down[i].astype(BF16))

    return xs.reshape(BATCH, SEQ, D_MODEL)
```

```python
import functools
import math

import jax
import jax.numpy as jnp
import numpy as np
from jax import lax
from jax.experimental import pallas as pl
from jax.experimental.pallas import tpu as pltpu

F32 = jnp.float32
BF16 = jnp.bfloat16

D_MODEL = 1024
BATCH = 16
SEQ = 2048
DEPTH = 4
GRID_W = 64
CTX_LEN = 256
A_WIDTH = 256
A_GROUPS = 4
CHUNK = 128
B_HEAD_DIM = 64
B_HEADS = 8
B_KV_HEADS = 2
WINDOW = 128
C_V_DIM = 64
C_QK_DIM = 32
C_HEADS = 4
PROJ_WIDTH = 2048
MIX_WIDTH = 1024
D_FF = 2816
ROPE_BASE = 10000.0
EPS = 1e-6

LOG2E = 1.4426950408889634
NEG_BIG = -1e30

LANES = 128
BF16_SUBLANES = 16

NLAT = BATCH * SEQ
NCTX = BATCH * CTX_LEN
NTOK = NLAT + NCTX
COND_ROWS = 24
CTX_COND_ROW = BATCH

PROJ_ROWS = 1024
ATT_ROWS = 256
ATT_BLOCK = 512
FFN_ROWS = 512
FF_TILE = 256
N_FF_TILES = D_FF // FF_TILE
ADA_TILE = 1536

OFF_AU, OFF_AV, OFF_BQ, OFF_BK, OFF_BV, OFF_CQ, OFF_CK, OFF_CV = (
    0, 256, 512, 1024, 1152, 1280, 1536, 1792)


def _params(sem, vmem_mb):
    return pltpu.CompilerParams(dimension_semantics=sem,
                                vmem_limit_bytes=vmem_mb * 1024 * 1024)


def _mod_kernel(cond_ref, w_ref, b_ref, o_ref):
    c = cond_ref[...]
    s = c / (1.0 + jnp.exp(-c))
    o_ref[0] = jnp.dot(s.astype(BF16), w_ref[0].astype(BF16),
                       preferred_element_type=F32) + b_ref[0]


def _modulation_all(cond, w_ada, b_ada):
    n_tiles = (6 * D_MODEL) // ADA_TILE
    return pl.pallas_call(
        _mod_kernel,
        grid=(DEPTH, n_tiles),
        in_specs=[
            pl.BlockSpec((COND_ROWS, D_MODEL), lambda l, j: (0, 0)),
            pl.BlockSpec((1, D_MODEL, ADA_TILE), lambda l, j: (l, 0, j)),
            pl.BlockSpec((1, 1, ADA_TILE), lambda l, j: (l, 0, j)),
        ],
        out_specs=pl.BlockSpec((1, COND_ROWS, ADA_TILE), lambda l, j: (l, 0, j)),
        out_shape=jax.ShapeDtypeStruct((DEPTH, COND_ROWS, 6 * D_MODEL), F32),
        compiler_params=_params(("arbitrary", "arbitrary"), 32),
        name="modulation",
    )(cond, w_ada, b_ada.reshape(DEPTH, 1, 6 * D_MODEL))


def _norm_modulate(x, gain, shift, scale):
    ms = jnp.mean(x * x, axis=-1, keepdims=True)
    return (x * lax.rsqrt(ms + EPS) * gain) * (1.0 + scale) + shift


def _gelu_tanh(x):
    return 0.5 * x * (1.0 + jnp.tanh(math.sqrt(2.0 / math.pi) * (x + 0.044715 * (x * x * x))))


def _group_mean_sq(t, g):
    return jnp.dot((t * t).astype(BF16), g, preferred_element_type=F32)


def _swap_halves(t, half):
    lane = lax.broadcasted_iota(jnp.int32, t.shape, 1)
    first = (lane & (2 * half - 1)) < half
    return jnp.where(first, pltpu.roll(t, LANES - half, 1), pltpu.roll(t, half, 1))


def _lane_iota(shape):
    return lax.broadcasted_iota(jnp.int32, shape, 1)


def _proj_kernel(x_ref, sh_ref, sc_ref, ng_ref, w_ref, g64_ref, g32_ref,
                 cosb_ref, sinb_ref, cosc_ref, sinc_ref, gains_ref,
                 u_ref, vn_ref, qb_ref, kbt_ref, vb_ref, qc_ref, kct_ref, vc_ref):
    h = _norm_modulate(x_ref[...], ng_ref[...], sh_ref[0], sc_ref[0])
    z = jnp.dot(h.astype(BF16), w_ref[...], preferred_element_type=F32)

    u_ref[...] = _gelu_tanh(z[:, OFF_AU:OFF_AU + A_WIDTH])
    v = _gelu_tanh(z[:, OFF_AV:OFF_AV + A_WIDTH])
    vn_ref[...] = (v * lax.rsqrt(_group_mean_sq(v, g64_ref[...]) + EPS)).astype(BF16)

    def norm_rope(t, ms, gain, cos_ref, sin_ref, half):
        tg = t * gain
        return lax.rsqrt(ms + EPS) * (tg * cos_ref[...] + _swap_halves(tg, half) * sin_ref[...])

    for t in range(4):
        lo = OFF_BQ + 256 * (t // 2)
        if t % 2 == 0:
            q2 = z[:, lo:lo + 256]
            ms2 = _group_mean_sq(q2, g64_ref[...])
        s = slice(LANES * (t % 2), LANES * (t % 2) + LANES)
        qb_ref[:, LANES * t:LANES * t + LANES] = norm_rope(
            q2[:, s], ms2[:, s], gains_ref[0:1, :], cosb_ref, sinb_ref, 16).astype(BF16)

    kb = z[:, OFF_BK:OFF_BK + LANES]
    kb = norm_rope(kb, _group_mean_sq(kb, g64_ref[0:LANES, 0:LANES]),
                   gains_ref[1:2, :], cosb_ref, sinb_ref, 16)
    kbt = kb.T.astype(BF16)
    kbt_ref[0:LANES, :] = kbt
    kbt_ref[LANES:LANES + 64, :] = kbt[64:LANES, :]
    kbt_ref[LANES + 64:2 * LANES, :] = kbt[0:64, :]

    vb = z[:, OFF_BV:OFF_BV + LANES]
    ones = jnp.ones((PROJ_ROWS, LANES), BF16)
    vb_ref[:, 0:LANES] = vb.astype(BF16)
    vb_ref[:, LANES:2 * LANES] = ones
    vb_ref[:, 2 * LANES:3 * LANES] = pltpu.roll(vb, 64, 1).astype(BF16)
    vb_ref[:, 3 * LANES:4 * LANES] = ones

    cq = z[:, OFF_CQ:OFF_CQ + 256]
    msq = _group_mean_sq(cq, g32_ref[...])
    ck = z[:, OFF_CK:OFF_CK + 256]
    msk = _group_mean_sq(ck, g32_ref[...])
    for t in range(2):
        s = slice(LANES * t, LANES * t + LANES)
        qc_ref[:, s] = norm_rope(cq[:, s], msq[:, s], gains_ref[2:3, :],
                                 cosc_ref, sinc_ref, 8).astype(BF16)
        kc = norm_rope(ck[:, s], msk[:, s], gains_ref[3:4, :], cosc_ref, sinc_ref, 8)
        kct_ref[s, :] = kc.T.astype(BF16)

    cv = z[:, OFF_CV:OFF_CV + 256]
    vc_ref[:, 0:LANES] = cv[:, 0:LANES].astype(BF16)
    vc_ref[:, LANES:2 * LANES] = ones
    vc_ref[:, 2 * LANES:3 * LANES] = cv[:, LANES:2 * LANES].astype(BF16)
    vc_ref[:, 3 * LANES:4 * LANES] = ones


def _proj(xs, mod, norm_g, w_in, consts, gains):
    n_lat_blocks = NLAT // PROJ_ROWS
    blocks_per_seq = SEQ // PROJ_ROWS

    def cond_row(i):
        return jnp.where(i < n_lat_blocks, i // blocks_per_seq, CTX_COND_ROW)

    def pos_block(i):
        return jnp.where(i < n_lat_blocks, i % blocks_per_seq, blocks_per_seq)

    row = lambda i: (i, 0)
    col = lambda i: (0, i)
    whole = lambda i: (0, 0)
    pos = lambda i: (pos_block(i), 0)
    out_shape = (
        jax.ShapeDtypeStruct((NTOK, 256), F32),
        jax.ShapeDtypeStruct((NTOK, 256), BF16),
        jax.ShapeDtypeStruct((NTOK, 512), BF16),
        jax.ShapeDtypeStruct((256, NTOK), BF16),
        jax.ShapeDtypeStruct((NTOK, 512), BF16),
        jax.ShapeDtypeStruct((NTOK, 256), BF16),
        jax.ShapeDtypeStruct((256, NTOK), BF16),
        jax.ShapeDtypeStruct((NTOK, 512), BF16),
    )
    out_specs = (
        pl.BlockSpec((PROJ_ROWS, 256), row),
        pl.BlockSpec((PROJ_ROWS, 256), row),
        pl.BlockSpec((PROJ_ROWS, 512), row),
        pl.BlockSpec((256, PROJ_ROWS), col),
        pl.BlockSpec((PROJ_ROWS, 512), row),
        pl.BlockSpec((PROJ_ROWS, 256), row),
        pl.BlockSpec((256, PROJ_ROWS), col),
        pl.BlockSpec((PROJ_ROWS, 512), row),
    )
    return pl.pallas_call(
        _proj_kernel,
        grid=(NTOK // PROJ_ROWS,),
        in_specs=[
            pl.BlockSpec((PROJ_ROWS, D_MODEL), row),
            pl.BlockSpec((1, 1, D_MODEL), lambda i: (cond_row(i) * 6 + 0, 0, 0)),
            pl.BlockSpec((1, 1, D_MODEL), lambda i: (cond_row(i) * 6 + 1, 0, 0)),
            pl.BlockSpec((1, D_MODEL), whole),
            pl.BlockSpec((D_MODEL, PROJ_WIDTH), whole),
            pl.BlockSpec((256, 256), whole),
            pl.BlockSpec((256, 256), whole),
            pl.BlockSpec((PROJ_ROWS, LANES), pos),
            pl.BlockSpec((PROJ_ROWS, LANES), pos),
            pl.BlockSpec((PROJ_ROWS, LANES), pos),
            pl.BlockSpec((PROJ_ROWS, LANES), pos),
            pl.BlockSpec((8, LANES), whole),
        ],
        out_specs=out_specs,
        out_shape=out_shape,
        compiler_params=_params(("arbitrary",), 48),
        name="proj",
    )(xs, mod, mod, norm_g, w_in, consts["g64"], consts["g32"],
      consts["cosb"], consts["sinb"], consts["cosc"], consts["sinc"], gains)


def _exp2_bf16(s, m):
    return jnp.exp2(s - m).astype(BF16)


def _row_max(s):
    return jnp.max(s, axis=-1, keepdims=True)


def _mixer_kernel(u_ref, vn_ref, qb_ref, qc_ref,
                  kbl_ref, kbc_ref, vbl_ref, vbc_ref,
                  kcl_ref, kcc_ref, vcl_ref, vcc_ref,
                  sink_ref, lam_ref, wcat_ref, abias_ref, subln_ref,
                  mix_ref, *, lam_init, steps_per_batch):
    s = pl.program_id(1)
    lane = _lane_iota((ATT_ROWS, LANES))
    low_half = lane < 64
    dot = functools.partial(jnp.dot, preferred_element_type=F32)

    def mixer_a(rb):
        for c in range(ATT_ROWS // CHUNK):
            rows = pl.ds(rb + CHUNK * c, CHUNK)
            for t in range(2):
                cols = slice(LANES * t, LANES * t + LANES)
                vt = vn_ref[rows, cols].astype(F32)
                keep_lo = _lane_iota(vt.shape) < 64
                rhs = jnp.concatenate([jnp.where(keep_lo, vt, 0.0), jnp.where(keep_lo, 0.0, vt)],
                                      axis=0).astype(BF16)
                mixed = dot(wcat_ref[t], rhs) + abias_ref[:, cols]
                mix_ref[rows, cols] = (u_ref[rows, cols] * mixed).astype(BF16)

    lp = lam_ref[...]
    lam = (jnp.exp(jnp.sum(lp[0:1] * lp[1:2], axis=-1, keepdims=True))
           - jnp.exp(jnp.sum(lp[2:3] * lp[3:4], axis=-1, keepdims=True)) + lam_init)

    def head_mask(offset, width):
        return jnp.where((lane >= offset) & (lane < offset + width), 1.0, 0.0)

    def softmax_pv(scores, values, sink2):
        m = _row_max(scores[0])
        for s in scores[1:]:
            m = jnp.maximum(m, _row_max(s))
        if sink2 is not None:
            m = jnp.maximum(m, sink2)
        oa = dot(_exp2_bf16(scores[0], m), values[0])
        for s, v in zip(scores[1:], values[1:]):
            oa = oa + dot(_exp2_bf16(s, m), v)
        den = oa[:, LANES:2 * LANES]
        if sink2 is not None:
            den = den + jnp.exp2(sink2 - m)
        return oa[:, 0:LANES] / den

    def mixer_b(latent, rb, q0):
        qrows = pl.ds(rb, ATT_ROWS)
        if latent:
            start = pl.multiple_of(jnp.clip(q0 - WINDOW, 0, SEQ - 2 * ATT_ROWS), LANES)
            qpos = q0 + lax.broadcasted_iota(jnp.int32, (ATT_ROWS, 2 * ATT_ROWS), 0)
            kpos = start + lax.broadcasted_iota(jnp.int32, (ATT_ROWS, 2 * ATT_ROWS), 1)
            valid = jnp.abs(qpos - kpos) <= WINDOW
        for p in range(B_HEADS // 2):
            qt = qb_ref[qrows, LANES * p:LANES * p + LANES].astype(F32)
            outs = []
            for r in range(2):
                h = 2 * p + r
                j = h // (B_HEADS // B_KV_HEADS)
                var = 0 if j == r else 1
                krows = slice(LANES * var, LANES * var + LANES)
                vcols = slice(256 * var, 256 * var + 256)
                qm = (qt * head_mask(64 * r, 64)).astype(BF16)
                sink2 = sink_ref[h] * LOG2E
                s_ctx = dot(qm, kbc_ref[krows, :])
                if latent:
                    s_win = dot(qm, kbl_ref[krows, pl.ds(start, 2 * ATT_ROWS)])
                    s_win = jnp.where(valid, s_win, NEG_BIG)
                    outs.append(softmax_pv([s_win, s_ctx],
                                           [vbl_ref[pl.ds(start, 2 * ATT_ROWS), vcols],
                                            vbc_ref[:, vcols]], sink2))
                else:
                    outs.append(softmax_pv([s_ctx], [vbc_ref[:, vcols]], sink2))
            mix_ref[qrows, A_WIDTH + LANES * p:A_WIDTH + LANES * p + LANES] = (
                jnp.where(low_half, outs[0], outs[1]).astype(BF16))

    def mixer_c(latent, rb):
        qrows = pl.ds(rb, ATT_ROWS)
        for t in range(C_HEADS // 2):
            qt = qc_ref[qrows, LANES * t:LANES * t + LANES].astype(F32)
            krows = slice(LANES * t, LANES * t + LANES)
            vcols = slice(256 * t, 256 * t + 256)
            outs = []
            for r in range(2):
                maps = []
                for mp in range(2):
                    qm = (qt * head_mask(64 * r + C_QK_DIM * mp, C_QK_DIM)).astype(BF16)
                    s_ctx = dot(qm, kcc_ref[krows, :])
                    if latent:
                        s_lat = dot(qm, kcl_ref[krows, :])
                        maps.append(softmax_pv([s_lat, s_ctx],
                                               [vcl_ref[:, vcols], vcc_ref[:, vcols]], None))
                    else:
                        maps.append(softmax_pv([s_ctx], [vcc_ref[:, vcols]], None))
                outs.append(maps[0] - lam * maps[1])
            o = jnp.where(low_half, outs[0], outs[1])
            o2 = o * o
            ms_lo = jnp.sum(jnp.where(low_half, o2, 0.0), axis=-1, keepdims=True) * (1.0 / C_V_DIM)
            ms_hi = jnp.sum(jnp.where(low_half, 0.0, o2), axis=-1, keepdims=True) * (1.0 / C_V_DIM)
            rs = jnp.where(low_half, lax.rsqrt(ms_lo + EPS), lax.rsqrt(ms_hi + EPS))
            off = A_WIDTH + 512 + LANES * t
            mix_ref[qrows, off:off + LANES] = (o * rs * subln_ref[...]).astype(BF16)

    @pl.when(s < 2 * steps_per_batch)
    def _():
        q_base = (s % steps_per_batch) * ATT_BLOCK
        subs = [(ATT_ROWS * k, q_base + ATT_ROWS * k) for k in range(ATT_BLOCK // ATT_ROWS)]
        for rb, _ in subs:
            mixer_c(True, rb)
        for rb, q0 in subs:
            mixer_b(True, rb, q0)
        for rb, _ in subs:
            mixer_a(rb)

    @pl.when(s >= 2 * steps_per_batch)
    def _():
        rb = pl.multiple_of((s - 2 * steps_per_batch) * CTX_LEN, CTX_LEN)
        mixer_c(False, rb)
        mixer_b(False, rb, None)
        mixer_a(rb)


def _mixers(u, vn, qb, kbt, vb, qc, kct, vc, sink, c_lam, wcat, abias, subln, lam_init, with_ctx):
    steps_per_batch = SEQ // ATT_BLOCK
    n_steps = 2 * steps_per_batch + (2 if with_ctx else 0)
    out_rows = NTOK if with_ctx else NLAT

    def second(s):
        is_ctx = s >= 2 * steps_per_batch
        return jnp.where(is_ctx, s - 2 * steps_per_batch, s // steps_per_batch)

    def qrow(p, s):
        latent = (2 * p + s // steps_per_batch) * steps_per_batch + s % steps_per_batch
        return (jnp.where(s < 2 * steps_per_batch, latent, NLAT // ATT_BLOCK + p), 0)

    lat_batch = lambda p, s: 2 * p + jnp.minimum(s // steps_per_batch, 1)
    lat_rows = lambda p, s: (lat_batch(p, s), 0)
    lat_cols = lambda p, s: (0, lat_batch(p, s))
    ctx_rows = lambda p, s: (NLAT // CTX_LEN + 2 * p + second(s), 0)
    ctx_cols = lambda p, s: (0, NLAT // CTX_LEN + 2 * p + second(s))
    whole2 = lambda p, s: (0, 0)
    kernel = functools.partial(_mixer_kernel, lam_init=lam_init, steps_per_batch=steps_per_batch)
    return pl.pallas_call(
        kernel,
        grid=(BATCH // 2, n_steps),
        in_specs=[
            pl.BlockSpec((ATT_BLOCK, 256), qrow),
            pl.BlockSpec((ATT_BLOCK, 256), qrow),
            pl.BlockSpec((ATT_BLOCK, 512), qrow),
            pl.BlockSpec((ATT_BLOCK, 256), qrow),
            pl.BlockSpec((256, SEQ), lat_cols),
            pl.BlockSpec((256, CTX_LEN), ctx_cols),
            pl.BlockSpec((SEQ, 512), lat_rows),
            pl.BlockSpec((CTX_LEN, 512), ctx_rows),
            pl.BlockSpec((256, SEQ), lat_cols),
            pl.BlockSpec((256, CTX_LEN), ctx_cols),
            pl.BlockSpec((SEQ, 512), lat_rows),
            pl.BlockSpec((CTX_LEN, 512), ctx_rows),
            pl.BlockSpec(memory_space=pltpu.SMEM),
            pl.BlockSpec((4, C_QK_DIM), whole2),
            pl.BlockSpec((2, CHUNK, 256), lambda p, s: (0, 0, 0)),
            pl.BlockSpec((CHUNK, 256), whole2),
            pl.BlockSpec((1, LANES), whole2),
        ],
        out_specs=pl.BlockSpec((ATT_BLOCK, MIX_WIDTH), qrow),
        out_shape=jax.ShapeDtypeStruct((out_rows, MIX_WIDTH), BF16),
        compiler_params=_params(("arbitrary", "arbitrary"), 48),
        name="mixers",
    )(u, vn, qb, qc, kbt, kbt, vb, vb, kct, kct, vc, vc, sink, c_lam, wcat, abias, subln)


HALO = BF16_SUBLANES


SEAM_PAD = 8


def _ffn_kernel(mix_ref, mixp_ref, mixn_ref, x_ref, xp_ref, xn_ref,
                g1_ref, sh_ref, sc_ref, g2_ref, ng_ref,
                wout_ref, wup_ref, cw_ref, cb_ref, wd_ref,
                o_ref, mext_ref, xext_ref, lhs_ref, *scratch, n_lat_blocks, blocks_per_seq):
    *a_refs, y_ref = scratch
    top = slice(0, HALO)
    mid = slice(HALO, HALO + FFN_ROWS)
    bot = slice(HALO + FFN_ROWS, 2 * HALO + FFN_ROWS)
    i = pl.program_id(0)
    is_ctx = i >= n_lat_blocks
    seq_pos = i % blocks_per_seq
    no_prev = jnp.logical_or(is_ctx, seq_pos == 0)
    no_next = jnp.logical_or(is_ctx, seq_pos == blocks_per_seq - 1)

    mext_ref[top, :] = mixp_ref[...]
    mext_ref[mid, :] = mix_ref[...]
    mext_ref[bot, :] = mixn_ref[...]
    xext_ref[top, :] = xp_ref[...]
    xext_ref[mid, :] = x_ref[...]
    xext_ref[bot, :] = xn_ref[...]

    x1 = xext_ref[...] + g1_ref[0] * jnp.dot(mext_ref[...], wout_ref[...],
                                             preferred_element_type=F32)
    xext_ref[...] = x1
    lhs_ref[...] = _norm_modulate(x1, ng_ref[...], sh_ref[0], sc_ref[0]).astype(BF16)
    zeros = jnp.zeros((HALO, D_MODEL), BF16)
    lhs_ref[top, :] = jnp.where(no_prev, zeros, lhs_ref[top, :])
    lhs_ref[bot, :] = jnp.where(no_next, zeros, lhs_ref[bot, :])

    def conv(a, pad, cols, mprev=None, mnext=None):
        n = a.shape[0]
        rows = slice(pad, n - pad)
        prev = pltpu.roll(a, 1, 0)[rows, :]
        nxt = pltpu.roll(a, n - 1, 0)[rows, :]
        if mprev is not None:
            prev = prev * mprev
            nxt = nxt * mnext
        return (prev * cw_ref[0:1, cols] + a[rows, :] * cw_ref[1:2, cols] + nxt * cw_ref[2:3, cols]
                + cb_ref[:, cols])

    def activate(gate, val):
        return ((gate / (1.0 + jnp.exp(-gate))) * val).astype(BF16)

    seams = []
    for seam in range(CTX_LEN, FFN_ROWS, CTX_LEN):
        lo = seam - BF16_SUBLANES
        row = lo + lax.broadcasted_iota(jnp.int32, (2 * BF16_SUBLANES, FF_TILE), 0)
        mprev = jnp.where(jnp.logical_and(is_ctx, row == seam), 0.0, 1.0)
        mnext = jnp.where(jnp.logical_and(is_ctx, row == seam - 1), 0.0, 1.0)
        seams.append((lo, mprev, mnext))

    for t in range(N_FF_TILES):
        gcols = slice(FF_TILE * t, FF_TILE * t + FF_TILE)
        vcols = slice(D_FF + FF_TILE * t, D_FF + FF_TILE * t + FF_TILE)
        ag_ref, av_ref = a_refs[2 * (t % 2)], a_refs[2 * (t % 2) + 1]
        ag_ref[...] = jnp.dot(lhs_ref[...], wup_ref[:, gcols], preferred_element_type=F32)
        av_ref[...] = jnp.dot(lhs_ref[...], wup_ref[:, vcols], preferred_element_type=F32)
        y_ref[:, gcols] = activate(conv(ag_ref[...], HALO, gcols), conv(av_ref[...], HALO, vcols))
        for lo, mprev, mnext in seams:
            slab = slice(HALO + lo - SEAM_PAD, HALO + lo + 2 * BF16_SUBLANES + SEAM_PAD)
            y_ref[lo:lo + 2 * BF16_SUBLANES, gcols] = activate(
                conv(ag_ref[slab, :], SEAM_PAD, gcols, mprev, mnext),
                conv(av_ref[slab, :], SEAM_PAD, vcols, mprev, mnext))

    acc = jnp.dot(y_ref[...], wd_ref[...], preferred_element_type=F32)
    o_ref[...] = xext_ref[mid, :] + g2_ref[0] * acc


def _ffn(mix, xs, mod, norm_g, w_out, w_up, conv_w, conv_b, w_down):
    n_rows = mix.shape[0]
    n_lat_blocks = NLAT // FFN_ROWS
    blocks_per_seq = SEQ // FFN_ROWS
    halo_per_block = FFN_ROWS // HALO
    ext_rows = FFN_ROWS + 2 * HALO

    row = lambda i: (i, 0)
    whole = lambda i: (0, 0)
    resident = dict(pipeline_mode=pl.Buffered(1))

    def cond_row(i):
        return jnp.where(i < n_lat_blocks, i // blocks_per_seq, CTX_COND_ROW)

    modspec = lambda k: pl.BlockSpec((1, 1, D_MODEL), lambda i: (cond_row(i) * 6 + k, 0, 0))
    main = pl.BlockSpec((FFN_ROWS, D_MODEL), row)
    prev_halo = pl.BlockSpec((HALO, D_MODEL),
                             lambda i: (jnp.maximum(i * halo_per_block - 1, 0), 0))
    next_halo = pl.BlockSpec((HALO, D_MODEL),
                             lambda i: (jnp.minimum((i + 1) * halo_per_block,
                                                    n_rows // HALO - 1), 0))
    in_specs = [
        main, prev_halo, next_halo, main, prev_halo, next_halo,
        modspec(2), modspec(3), modspec(4), modspec(5), pl.BlockSpec((1, D_MODEL), whole),
        pl.BlockSpec((MIX_WIDTH, D_MODEL), whole, **resident),
        pl.BlockSpec((D_MODEL, 2 * D_FF), whole, **resident),
        pl.BlockSpec((3, 2 * D_FF), whole, **resident),
        pl.BlockSpec((1, 2 * D_FF), whole, **resident),
        pl.BlockSpec((D_FF, D_MODEL), whole, **resident),
    ]
    a_shape = pltpu.VMEM((ext_rows, FF_TILE), F32)
    return pl.pallas_call(
        functools.partial(_ffn_kernel, n_lat_blocks=n_lat_blocks, blocks_per_seq=blocks_per_seq),
        grid=(n_rows // FFN_ROWS,),
        in_specs=in_specs,
        out_specs=main,
        out_shape=jax.ShapeDtypeStruct((n_rows, D_MODEL), F32),
        scratch_shapes=[pltpu.VMEM((ext_rows, MIX_WIDTH), BF16),
                        pltpu.VMEM((ext_rows, D_MODEL), F32),
                        pltpu.VMEM((ext_rows, D_MODEL), BF16),
                        a_shape, a_shape, a_shape, a_shape,
                        pltpu.VMEM((FFN_ROWS, D_FF), BF16)],
        compiler_params=_params(("arbitrary",), 56),
        name="ffn",
    )(mix, mix, mix, xs, xs, xs, mod, mod, mod, mod, norm_g, w_out, w_up, conv_w, conv_b, w_down)


def _rope_tables(head_dim):
    n_freq = head_dim // 4
    rows = SEQ // GRID_W
    row = jnp.repeat(jnp.arange(rows, dtype=F32), GRID_W)
    col = jnp.tile(jnp.arange(GRID_W, dtype=F32), rows)
    inv_freq = ROPE_BASE ** (-jnp.arange(n_freq, dtype=F32) / n_freq)
    ang = jnp.stack([row[:, None] * inv_freq, col[:, None] * inv_freq], axis=1)
    lane = np.arange(LANES)
    within = lane % head_dim
    axis = within // (2 * n_freq)
    second = (within % (2 * n_freq)) >= n_freq
    ang = ang[:, axis, lane % n_freq]
    cos = jnp.cos(ang)
    sin = jnp.where(jnp.asarray(second)[None, :], jnp.sin(ang), -jnp.sin(ang))
    cos = jnp.concatenate([cos, jnp.ones((PROJ_ROWS, LANES), F32)], axis=0)
    sin = jnp.concatenate([sin, jnp.zeros((PROJ_ROWS, LANES), F32)], axis=0)
    return cos, sin


def _group_matrix(width):
    idx = np.arange(256)
    g = (idx[:, None] // width == idx[None, :] // width).astype(np.float32) / width
    return jnp.asarray(g, dtype=BF16)


def _constants():
    cosb, sinb = _rope_tables(B_HEAD_DIM)
    cosc, sinc = _rope_tables(C_QK_DIM)
    return dict(cosb=cosb, sinb=sinb, cosc=cosc, sinc=sinc,
                g64=_group_matrix(64), g32=_group_matrix(32))


def kernel(x, c, ctx, c_ctx, w_ada, b_ada, norm1_g, norm2_g, w_in, a_ws, a_bs, b_qnorm, b_knorm,
           b_sink, c_qnorm, c_knorm, c_lam, c_subln, w_out, w_up, conv_w, conv_b, w_down):
    consts = _constants()
    xs = jnp.concatenate([x.reshape(NLAT, D_MODEL), ctx.reshape(NCTX, D_MODEL)], axis=0)

    cond = jnp.zeros((COND_ROWS, D_MODEL), F32).at[:BATCH].set(c).at[CTX_COND_ROW].set(c_ctx)
    mod_all = _modulation_all(cond, w_ada, b_ada)
    mod_all = mod_all.reshape(DEPTH, COND_ROWS * 6, 1, D_MODEL)

    for i in range(DEPTH):
        last = i == DEPTH - 1
        lam_init = 0.8 - 0.6 * math.exp(-0.3 * i)
        mod = mod_all[i]
        zeros = jnp.zeros((4, LANES), F32)
        gains = jnp.concatenate([
            jnp.tile(b_qnorm[i] * (B_HEAD_DIM ** -0.5 * LOG2E), LANES // B_HEAD_DIM)[None],
            jnp.tile(b_knorm[i], LANES // B_HEAD_DIM)[None],
            jnp.tile(c_qnorm[i] * (C_QK_DIM ** -0.5 * LOG2E), LANES // C_QK_DIM)[None],
            jnp.tile(c_knorm[i], LANES // C_QK_DIM)[None],
            zeros], axis=0)
        u, vn, qb, kbt, vb, qc, kct, vc = _proj(
            xs, mod, norm1_g[i][None], w_in[i].astype(BF16), consts, gains)

        wcat = jnp.concatenate([a_ws[i][0::2], a_ws[i][1::2]], axis=-1).astype(BF16)
        abias = jnp.repeat(a_bs[i].T, A_WIDTH // A_GROUPS, axis=1)
        subln = jnp.tile(c_subln[i] * (1.0 - lam_init), LANES // C_V_DIM)[None]
        mix = _mixers(u, vn, qb, kbt, vb, qc, kct, vc, b_sink[i], c_lam[i], wcat, abias, subln,
                      lam_init, with_ctx=not last)

        xs = _ffn(mix, xs, mod, norm2_g[i][None], w_out[i].astype(BF16), w_up[i].astype(BF16),
                  conv_w[i].reshape(3, 2 * D_FF), conv_b[i][None], w_down[i].astype(BF16))

    return xs.reshape(BATCH, SEQ, D_MODEL)
```
